```python
import jax, jax.numpy as jnp
from jax import lax
import numpy as np

D_MODEL = 1024
BATCH = 16
SEQ = 4096
DEPTH = 4
DEC_BATCH = 32
DEC_SEQ = 16
PAST_LEN = 1024

CHUNK = 64
N_A = DEPTH // 2
N_B = DEPTH - N_A
GLA_HEADS = 4
GLA_DK = D_MODEL // 2 // GLA_HEADS
GLA_DV = D_MODEL // GLA_HEADS
GLA_RANK = 16
GLA_TAU = 16.0
GLA_BLOCK = 16
QK_W = GLA_HEADS * GLA_DK
V_W = GLA_HEADS * GLA_DV
GLA_IN = 2 * QK_W + 2 * V_W + GLA_RANK
ATT_HEADS = 16
ATT_KV_HEADS = 2
ATT_GROUP = ATT_HEADS // ATT_KV_HEADS
HEAD_DIM = D_MODEL // ATT_HEADS
KV_W = ATT_KV_HEADS * HEAD_DIM
WINDOW = 128
D_FF = 2816
CONV_W = 3
EPS = 1e-6

kernel_name = 'yoco_gla_swa_sink_convffn_stream_step'


def rmsnorm(x, g):
    xf = x.astype(jnp.float32)
    y = xf * lax.rsqrt(jnp.mean(xf * xf, -1, keepdims=True) + EPS)
    return (y * g.astype(jnp.float32)).astype(x.dtype)


def gla_mixer(h, s0, w_in, w_g2, b_g2, g_onorm, w_out):
    B, T, _ = h.shape
    f32 = jnp.float32
    p = h @ w_in
    q, k, v, r, gl = jnp.split(p, [QK_W, 2 * QK_W, 2 * QK_W + V_W, 2 * QK_W + 2 * V_W], axis=-1)
    log_a = jax.nn.log_sigmoid((gl @ w_g2 + b_g2).astype(f32)) / GLA_TAU
    pad = (-T) % GLA_BLOCK
    nb = (T + pad) // GLA_BLOCK

    def blocks(a, d):
        a = a.astype(f32).reshape(B, T, GLA_HEADS, d)
        a = jnp.pad(a, ((0, 0), (0, pad), (0, 0), (0, 0)))
        return a.reshape(B, nb, GLA_BLOCK, GLA_HEADS, d).transpose(1, 0, 3, 2, 4)

    qb = blocks(q, GLA_DK) * (GLA_DK ** -0.5)
    kb = blocks(k, GLA_DK)
    vb = blocks(v, GLA_DV)
    ab = blocks(log_a, GLA_DK)
    tril = jnp.tril(jnp.ones((GLA_BLOCK, GLA_BLOCK), dtype=bool))

    def step(S, blk):
        qc, kc, vc, ac = blk
        bcum = jnp.cumsum(ac, axis=2)
        qd = qc * jnp.exp(bcum)
        kd = kc * jnp.exp(-bcum)
        att = jnp.where(tril, jnp.einsum('bhld,bhmd->bhlm', qd, kd), 0.0)
        o = jnp.einsum('bhld,bhde->bhle', qd, S) + jnp.einsum('bhlm,bhme->bhle', att, vc)
        blast = bcum[:, :, -1:, :]
        kr = kc * jnp.exp(blast - bcum)
        S = S * jnp.exp(blast[:, :, 0, :])[..., None] + jnp.einsum('bhld,bhle->bhde', kr, vc)
        return S, o

    s_final, o = lax.scan(step, s0.astype(f32), (qb, kb, vb, ab))
    o = o.transpose(1, 0, 3, 2, 4).reshape(B, nb * GLA_BLOCK, GLA_HEADS, GLA_DV)[:, :T]
    o = o * lax.rsqrt(jnp.mean(o * o, -1, keepdims=True) + EPS) * g_onorm.astype(f32)
    o = o.reshape(B, T, V_W) * jax.nn.silu(r.astype(f32))
    return o.astype(h.dtype) @ w_out, s_final


def sink_attention(q, k, v, valid, sinks):
    s = jnp.einsum('bqngd,bknd->bngqk', q, k).astype(jnp.float32)
    s = jnp.where(valid, s, -jnp.inf)
    sk = sinks.astype(jnp.float32).reshape(1, ATT_KV_HEADS, ATT_GROUP, 1, 1)
    m = jnp.maximum(jnp.max(s, -1, keepdims=True), sk)
    e = jnp.exp(s - m)
    p = e / (jnp.sum(e, -1, keepdims=True) + jnp.exp(sk - m))
    return jnp.einsum('bngqk,bknd->bqngd', p.astype(v.dtype), v)


def swa_prompt(h, k, v, w_q, sinks, w_o):
    B, T, _ = h.shape
    q = (h @ w_q).reshape(B, T, ATT_KV_HEADS, ATT_GROUP, HEAD_DIM) * (HEAD_DIM ** -0.5)
    n_chunks = T // CHUNK
    band = WINDOW + CHUNK
    kp = jnp.pad(k, ((0, 0), (WINDOW, 0), (0, 0), (0, 0)))
    vp = jnp.pad(v, ((0, 0), (WINDOW, 0), (0, 0), (0, 0)))

    def one_chunk(c):
        qs = lax.dynamic_slice_in_dim(q, c * CHUNK, CHUNK, axis=1)
        ks = lax.dynamic_slice_in_dim(kp, c * CHUNK, band, axis=1)
        vs = lax.dynamic_slice_in_dim(vp, c * CHUNK, band, axis=1)
        valid = (c * CHUNK - WINDOW + jnp.arange(band)) >= 0
        return sink_attention(qs, ks, vs, valid, sinks)

    o = lax.map(one_chunk, jnp.arange(n_chunks))
    o = jnp.moveaxis(o, 0, 1).reshape(B, T, ATT_HEADS * HEAD_DIM)
    return o @ w_o


def swa_sample(h, k_all, v_all, w_q, sinks, w_o):
    B, T, _ = h.shape
    q = (h @ w_q).reshape(B, T, ATT_KV_HEADS, ATT_GROUP, HEAD_DIM) * (HEAD_DIM ** -0.5)
    valid = jnp.ones((k_all.shape[1],), dtype=bool)
    o = sink_attention(q, k_all, v_all, valid, sinks).reshape(B, T, ATT_HEADS * HEAD_DIM)
    return o @ w_o


def conv_ffn(h, buf, w_up, conv_w, conv_b, w_down):
    T = h.shape[1]
    u = h @ w_up
    ue = jnp.concatenate([buf.astype(u.dtype), u], axis=1)
    c = ue[:, 0:T] * conv_w[0]
    for j in range(1, CONV_W):
        c = c + ue[:, j:j + T] * conv_w[j]
    c = c + conv_b
    g, val = jnp.split(c, 2, axis=-1)
    y = (jax.nn.silu(g) * val) @ w_down
    return y, ue[:, -(CONV_W - 1):]


def trunk(x, gla_s, ffn_buf, kv_cache_k, kv_cache_v, norm_mix, norm_ffn, norm_kv, norm_final,
          gla_w_in, gla_w_g2, gla_b_g2, gla_g_onorm, gla_w_out, w_kv, att_w_q, att_sinks, att_w_o,
          ffn_w_up, ffn_conv_w, ffn_conv_b, ffn_w_down):
    B, T, _ = x.shape
    new_gla, new_buf = [], []
    k_all = v_all = None
    for layer in range(DEPTH):
        h = rmsnorm(x, norm_mix[layer])
        if layer < N_A:
            y, s = gla_mixer(h, gla_s[layer], gla_w_in[layer], gla_w_g2[layer], gla_b_g2[layer],
                             gla_g_onorm[layer], gla_w_out[layer])
            new_gla.append(s)
        else:
            j = layer - N_A
            if kv_cache_k is None:
                y = swa_prompt(h, k_all, v_all, att_w_q[j], att_sinks[j], att_w_o[j])
            else:
                y = swa_sample(h, k_all, v_all, att_w_q[j], att_sinks[j], att_w_o[j])
        x = x + y
        y, b = conv_ffn(rmsnorm(x, norm_ffn[layer]), ffn_buf[layer], ffn_w_up[layer],
                        ffn_conv_w[layer], ffn_conv_b[layer], ffn_w_down[layer])
        new_buf.append(b)
        x = x + y
        if layer == N_A - 1:
            kv = rmsnorm(x, norm_kv) @ w_kv
            k_new, v_new = jnp.split(kv, 2, axis=-1)
            k_new = k_new.reshape(B, T, ATT_KV_HEADS, HEAD_DIM)
            v_new = v_new.reshape(B, T, ATT_KV_HEADS, HEAD_DIM)
            if kv_cache_k is None:
                k_all, v_all, keep = k_new, v_new, WINDOW
            else:
                k_all = jnp.concatenate([kv_cache_k.astype(k_new.dtype), k_new], axis=1)
                v_all = jnp.concatenate([kv_cache_v.astype(v_new.dtype), v_new], axis=1)
                keep = kv_cache_k.shape[1]
    y = rmsnorm(x, norm_final)
    return y, jnp.stack(new_gla), k_all[:, -keep:], v_all[:, -keep:], jnp.stack(new_buf)


def setup_inputs(seed: int = 0) -> dict:
    key = jax.random.key(seed)
    ks = jax.random.split(key, 32)
    f32 = jnp.float32

    def nrm(k, shape, scale=1.0):
        return jax.random.normal(k, shape, f32) * scale

    win_rows = min(WINDOW, PAST_LEN)
    return {
        'x_prompt': nrm(ks[0], (BATCH, SEQ, D_MODEL)),
        'x_sample': nrm(ks[1], (DEC_BATCH, DEC_SEQ, D_MODEL)),
        'state_gla': nrm(ks[2], (N_A, DEC_BATCH, GLA_HEADS, GLA_DK, GLA_DV)),
        'cache_k': nrm(ks[3], (DEC_BATCH, win_rows, ATT_KV_HEADS, HEAD_DIM)),
        'cache_v': nrm(ks[4], (DEC_BATCH, win_rows, ATT_KV_HEADS, HEAD_DIM)),
        'state_ffn_conv': nrm(ks[5], (DEPTH, DEC_BATCH, CONV_W - 1, 2 * D_FF)),
        'norm_mix': 1.0 + nrm(ks[6], (DEPTH, D_MODEL), 0.05),
        'norm_ffn': 1.0 + nrm(ks[7], (DEPTH, D_MODEL), 0.05),
        'norm_kv': 1.0 + nrm(ks[8], (D_MODEL,), 0.05),
        'norm_final': 1.0 + nrm(ks[9], (D_MODEL,), 0.05),
        'gla_w_in': nrm(ks[10], (N_A, D_MODEL, GLA_IN), D_MODEL ** -0.5),
        'gla_w_g2': nrm(ks[11], (N_A, GLA_RANK, QK_W), GLA_RANK ** -0.5),
        'gla_b_g2': nrm(ks[12], (N_A, QK_W), 0.1),
        'gla_g_onorm': 1.0 + nrm(ks[13], (N_A, GLA_DV), 0.05),
        'gla_w_out': nrm(ks[14], (N_A, V_W, D_MODEL), V_W ** -0.5),
        'w_kv': nrm(ks[15], (D_MODEL, 2 * KV_W), D_MODEL ** -0.5),
        'att_w_q': nrm(ks[16], (N_B, D_MODEL, ATT_HEADS * HEAD_DIM), D_MODEL ** -0.5),
        'att_sinks': nrm(ks[17], (N_B, ATT_HEADS), 1.0),
        'att_w_o': nrm(ks[18], (N_B, ATT_HEADS * HEAD_DIM, D_MODEL), (ATT_HEADS * HEAD_DIM) ** -0.5),
        'ffn_w_up': nrm(ks[19], (DEPTH, D_MODEL, 2 * D_FF), D_MODEL ** -0.5),
        'ffn_conv_w': nrm(ks[20], (DEPTH, CONV_W, 2 * D_FF), CONV_W ** -0.5),
        'ffn_conv_b': nrm(ks[21], (DEPTH, 2 * D_FF), 0.02),
        'ffn_w_down': nrm(ks[22], (DEPTH, D_FF, D_MODEL), D_FF ** -0.5),
    }


def reference(x_prompt, x_sample, state_gla, cache_k, cache_v, state_ffn_conv, norm_mix, norm_ffn, norm_kv,
              norm_final, gla_w_in, gla_w_g2, gla_b_g2, gla_g_onorm, gla_w_out, w_kv, att_w_q, att_sinks,
              att_w_o, ffn_w_up, ffn_conv_w, ffn_conv_b, ffn_w_down):
    bp = x_prompt.shape[0]
    gla0 = jnp.zeros((N_A, bp, GLA_HEADS, GLA_DK, GLA_DV), jnp.float32)
    buf0 = jnp.zeros((DEPTH, bp, CONV_W - 1, 2 * D_FF), x_prompt.dtype)
    y_prompt, gla_p, k_p, v_p, buf_p = trunk(
        x_prompt, gla0, buf0, None, None, norm_mix, norm_ffn, norm_kv, norm_final,
        gla_w_in, gla_w_g2, gla_b_g2, gla_g_onorm, gla_w_out, w_kv, att_w_q, att_sinks, att_w_o,
        ffn_w_up, ffn_conv_w, ffn_conv_b, ffn_w_down)
    y_sample, gla_s, k_s, v_s, buf_s = trunk(
        x_sample, state_gla, state_ffn_conv, cache_k, cache_v, norm_mix, norm_ffn, norm_kv, norm_final,
        gla_w_in, gla_w_g2, gla_b_g2, gla_g_onorm, gla_w_out, w_kv, att_w_q, att_sinks, att_w_o,
        ffn_w_up, ffn_conv_w, ffn_conv_b, ffn_w_down)
    return (y_prompt, y_sample, gla_p, k_p, v_p, buf_p, gla_s, k_s, v_s, buf_s)
```

```python
import functools

import jax
import jax.numpy as jnp
from jax import lax
from jax.experimental import pallas as pl
from jax.experimental.pallas import tpu as pltpu

F32 = jnp.float32
BF16 = jnp.bfloat16

D_MODEL = 1024
DEPTH = 4
N_A = DEPTH // 2
CHUNK = 64
WINDOW = 128
GLA_HEADS = 4
GLA_DK = 128
GLA_DV = 256
GLA_RANK = 16
GLA_TAU = 16.0
GLA_BLOCK = 16
GLA_CHUNK = 128
QK_W = GLA_HEADS * GLA_DK
V_W = GLA_HEADS * GLA_DV
ATT_HEADS = 16
ATT_KV_HEADS = 2
ATT_GROUP = ATT_HEADS // ATT_KV_HEADS
HEAD_DIM = 64
KV_W = ATT_KV_HEADS * HEAD_DIM
D_FF = 2816
CONV_W = 3
EPS = 1e-6

VMEM_LIMIT_BYTES = 56 * 1024 * 1024


def _rms(x, g):
    ms = jnp.mean(x * x, axis=-1, keepdims=True)
    return x * lax.rsqrt(ms + EPS) * g


def _dot(a, b):
    return jnp.dot(a, b, preferred_element_type=F32)


def _const_spec(shape):
    return pl.BlockSpec(shape, lambda b, t: (0,) * len(shape), pipeline_mode=pl.Buffered(1))


def _params():
    return pltpu.CompilerParams(dimension_semantics=("arbitrary", "arbitrary"),
                                vmem_limit_bytes=VMEM_LIMIT_BYTES)


def _gla_body(x_ref, s0_ref, g_ref, win_ref, wgl_ref, wg2_ref, bg2_ref, gon_ref, wout_ref,
              xo_ref, s_ref, *, tm):
    C = GLA_CHUNK
    L = GLA_BLOCK

    @pl.when(pl.program_id(1) == 0)
    def _init():
        s_ref[...] = s0_ref[...]

    x = x_ref[0]
    h = _rms(x, g_ref[...]).astype(BF16)
    p = _dot(h, win_ref[...])
    gl = _dot(h, wgl_ref[...])
    z = _dot(gl.astype(BF16), wg2_ref[...]) + bg2_ref[...]
    la = jax.nn.log_sigmoid(z) * (1.0 / GLA_TAU)

    rows = max(tm, C)
    if tm < C:
        zpad = lambda a: jnp.concatenate([a, jnp.zeros((C - tm, a.shape[1]), F32)], axis=0)
        p, la = zpad(p), zpad(la)

    ri = lax.broadcasted_iota(jnp.int32, (C, C), 0)
    ci = lax.broadcasted_iota(jnp.int32, (C, C), 1)
    tril = ri >= ci
    tril_f = tril.astype(F32)
    gon = gon_ref[...]

    for c in range(rows // C):
        sl = slice(c * C, (c + 1) * C)
        q = p[sl, 0:QK_W] * (GLA_DK ** -0.5)
        k = p[sl, QK_W:2 * QK_W]
        v = p[sl, 2 * QK_W:2 * QK_W + V_W]
        r = p[sl, 2 * QK_W + V_W:2 * QK_W + 2 * V_W]
        b = jnp.dot(tril_f, la[sl], precision=lax.Precision.HIGHEST, preferred_element_type=F32)
        b_end = b[C - 1:C]
        qt = (q * jnp.exp(b)).astype(BF16)
        kh = (k * jnp.exp(b_end - b)).astype(BF16)
        vb = v.astype(BF16)

        a_rows = []
        for i in range(C // L):
            b_i0 = b[L * i - 1:L * i] if i > 0 else jnp.zeros((1, QK_W), F32)
            qi = (q[L * i:L * (i + 1)] * jnp.exp(b[L * i:L * (i + 1)] - b_i0)).astype(BF16)
            n = L * (i + 1)
            ki = (k[:n] * jnp.exp(b_i0 - b[:n])).astype(BF16)
            if n < C:
                ki = jnp.concatenate([ki, jnp.zeros((C - n, QK_W), BF16)], axis=0)
            a_rows.append([
                lax.dot_general(qi[:, hd * GLA_DK:(hd + 1) * GLA_DK], ki[:, hd * GLA_DK:(hd + 1) * GLA_DK],
                                (((1,), (1,)), ((), ())), preferred_element_type=F32)
                for hd in range(GLA_HEADS)])

        og = []
        for hd in range(GLA_HEADS):
            ks = slice(hd * GLA_DK, (hd + 1) * GLA_DK)
            vs = slice(hd * GLA_DV, (hd + 1) * GLA_DV)
            att = jnp.concatenate([a_rows[i][hd] for i in range(C // L)], axis=0)
            att = jnp.where(tril, att, 0.0).astype(BF16)
            s_old = s_ref[0, hd]
            o = _dot(qt[:, ks], s_old.astype(BF16)) + _dot(att, vb[:, vs])
            ktv = lax.dot_general(kh[:, ks], vb[:, vs], (((0,), (0,)), ((), ())), preferred_element_type=F32)
            g_t = jnp.transpose(jnp.broadcast_to(jnp.exp(b_end[:, ks]), (GLA_DK, GLA_DK)))
            s_ref[0, hd] = s_old * jnp.concatenate([g_t, g_t], axis=1) + ktv
            o = o * lax.rsqrt(jnp.mean(o * o, axis=-1, keepdims=True) + EPS) * gon
            og.append((o * jax.nn.silu(r[:, vs])).astype(BF16))
        y = _dot(jnp.concatenate(og, axis=1), wout_ref[...])
        if tm < C:
            xo_ref[0] = x + y[:tm]
        else:
            xo_ref[0, sl, :] = x[sl] + y


def _gla_layer(x, s0, g, w_in, w_gl, w_g2, b_g2, g_on, w_out, *, tm):
    B, T, D = x.shape
    st_spec = pl.BlockSpec((1, GLA_HEADS, GLA_DK, GLA_DV), lambda b, t: (b, 0, 0, 0))
    return pl.pallas_call(
        functools.partial(_gla_body, tm=tm),
        grid=(B, T // tm),
        in_specs=[
            pl.BlockSpec((1, tm, D), lambda b, t: (b, t, 0)),
            st_spec,
            _const_spec((1, D)),
            _const_spec(w_in.shape),
            _const_spec(w_gl.shape),
            _const_spec(w_g2.shape),
            _const_spec((1, QK_W)),
            _const_spec((1, GLA_DV)),
            _const_spec(w_out.shape),
        ],
        out_specs=[pl.BlockSpec((1, tm, D), lambda b, t: (b, t, 0)), st_spec],
        out_shape=[jax.ShapeDtypeStruct((B, T, D), F32),
                   jax.ShapeDtypeStruct((B, GLA_HEADS, GLA_DK, GLA_DV), F32)],
        compiler_params=_params(),
        name="gla_layer",
    )(x, s0, g.reshape(1, D), w_in, w_gl, w_g2, b_g2.reshape(1, QK_W), g_on.reshape(1, GLA_DV), w_out)


def _ffn_body(*refs, tm, with_kv, with_final):
    x_ref, buf_ref, g_ref, wup_ref, cw_ref, cb_ref, wdn_ref = refs[:7]
    rest = list(refs[7:])
    if with_kv:
        gkv_ref, wkv_ref = rest[:2]
        rest = rest[2:]
    if with_final:
        gfin_ref = rest[0]
        rest = rest[1:]
    xo_ref, tail_ref = rest[:2]
    if with_kv:
        kv_ref = rest[2]

    @pl.when(pl.program_id(1) == 0)
    def _init():
        tail_ref[...] = buf_ref[...]

    x = x_ref[0]
    h = _rms(x, g_ref[...]).astype(BF16)
    row8 = lax.broadcasted_iota(jnp.int32, (8, D_FF), 0)
    halves = []
    for half in range(2):
        cs = slice(half * D_FF, (half + 1) * D_FF)
        u = _dot(h, wup_ref[:, cs])
        hist = tail_ref[0, :, cs]
        r1 = pltpu.roll(u, 1, 0)
        r2 = pltpu.roll(u, 2, 0)
        top1 = jnp.where(row8 == 0, hist[1:2], r1[:8])
        top2 = jnp.where(row8 == 0, hist[0:1], jnp.where(row8 == 1, hist[1:2], r2[:8]))
        if tm > 8:
            r1 = jnp.concatenate([top1, r1[8:]], axis=0)
            r2 = jnp.concatenate([top2, r2[8:]], axis=0)
        else:
            r1, r2 = top1, top2
        tail_ref[0, :, cs] = u[tm - 2:tm]
        cw = cw_ref[:, cs]
        halves.append(r2 * cw[0:1] + r1 * cw[1:2] + u * cw[2:3] + cb_ref[:, cs])
    act = (jax.nn.silu(halves[0]) * halves[1]).astype(BF16)
    xo = x + _dot(act, wdn_ref[...])
    if with_kv:
        kv_ref[0] = _dot(_rms(xo, gkv_ref[...]).astype(BF16), wkv_ref[...])
    if with_final:
        xo = _rms(xo, gfin_ref[...])
    xo_ref[0] = xo


def _ffn_layer(x, buf, g, w_up, conv_w, conv_b, w_down, *, tm, kv=None, final=None):
    B, T, D = x.shape
    tail_spec = pl.BlockSpec((1, CONV_W - 1, 2 * D_FF), lambda b, t: (b, 0, 0))
    x_spec = pl.BlockSpec((1, tm, D), lambda b, t: (b, t, 0))
    args = [x, buf, g.reshape(1, D), w_up, conv_w, conv_b.reshape(1, 2 * D_FF), w_down]
    in_specs = [x_spec, tail_spec, _const_spec((1, D)), _const_spec(w_up.shape), _const_spec(conv_w.shape),
                _const_spec((1, 2 * D_FF)), _const_spec(w_down.shape)]
    out_specs = [x_spec, tail_spec]
    out_shape = [jax.ShapeDtypeStruct((B, T, D), F32), jax.ShapeDtypeStruct((B, CONV_W - 1, 2 * D_FF), F32)]
    if kv is not None:
        g_kv, w_kv = kv
        args += [g_kv.reshape(1, D), w_kv]
        in_specs += [_const_spec((1, D)), _const_spec(w_kv.shape)]
        out_specs.append(pl.BlockSpec((1, tm, 2 * KV_W), lambda b, t: (b, t, 0)))
        out_shape.append(jax.ShapeDtypeStruct((B, T, 2 * KV_W), F32))
    if final is not None:
        args.append(final.reshape(1, D))
        in_specs.append(_const_spec((1, D)))
    return pl.pallas_call(
        functools.partial(_ffn_body, tm=tm, with_kv=kv is not None, with_final=final is not None),
        grid=(B, T // tm),
        in_specs=in_specs,
        out_specs=out_specs,
        out_shape=out_shape,
        compiler_params=_params(),
        name="conv_ffn",
    )(*args)


def _swa_body(*refs, tq, banded):
    if banded:
        x_ref, kv_ref, g_ref, wq_ref, sink_ref, wo_ref, xo_ref, kvbuf = refs
    else:
        x_ref, kv_ref, past_ref, g_ref, wq_ref, sink_ref, wo_ref, xo_ref, kvbuf = refs
    P = WINDOW
    t = pl.program_id(1)

    if banded:
        @pl.when(t == 0)
        def _first():
            kvbuf[0:P, :] = jnp.zeros((P, 2 * KV_W), BF16)

        @pl.when(t > 0)
        def _carry():
            kvbuf[0:P, :] = kvbuf[tq:tq + P, :]
    else:
        kvbuf[0:P, :] = past_ref[0].astype(BF16)
    kvbuf[P:P + tq, :] = kv_ref[0].astype(BF16)

    x = x_ref[0]
    h = _rms(x, g_ref[...]).astype(BF16)
    q = (_dot(h, wq_ref[...]) * (HEAD_DIM ** -0.5)).astype(BF16)

    sub = 2 * CHUNK if banded else tq
    outs = []
    for i in range(tq // sub):
        qs = q[i * sub:(i + 1) * sub]
        if banded:
            kv = kvbuf[i * sub:i * sub + sub + P, :]
            ri = lax.broadcasted_iota(jnp.int32, (sub, sub + P), 0)
            ci = lax.broadcasted_iota(jnp.int32, (sub, sub + P), 1)
            off = ci - jnp.where(ri >= CHUNK, CHUNK, 0)
            valid = (off >= 0) & (off < WINDOW + CHUNK) & (ci + (t * tq + i * sub - P) >= 0)
        else:
            kv = kvbuf[...]
            valid = None
        heads = []
        for hh in range(ATT_HEADS):
            n = hh // ATT_GROUP
            qh = qs[:, hh * HEAD_DIM:(hh + 1) * HEAD_DIM]
            kh = kv[:, n * HEAD_DIM:(n + 1) * HEAD_DIM]
            vh = kv[:, KV_W + n * HEAD_DIM:KV_W + (n + 1) * HEAD_DIM]
            s = lax.dot_general(qh, kh, (((1,), (1,)), ((), ())), preferred_element_type=F32)
            if valid is not None:
                s = jnp.where(valid, s, -jnp.inf)
            sk = sink_ref[hh]
            m = jnp.maximum(jnp.max(s, axis=-1, keepdims=True), sk)
            e = jnp.exp(s - m)
            den = jnp.sum(e, axis=-1, keepdims=True) + jnp.exp(sk - m)
            heads.append(_dot(e.astype(BF16), vh) / den)
        outs.append(jnp.concatenate(heads, axis=1).astype(BF16))
    o = outs[0] if len(outs) == 1 else jnp.concatenate(outs, axis=0)
    xo_ref[0] = x + _dot(o, wo_ref[...])


def _swa_layer(x, kv, past, g, w_q, sinks, w_o, *, tq):
    B, T, D = x.shape
    banded = past is None
    x_spec = pl.BlockSpec((1, tq, D), lambda b, t: (b, t, 0))
    args = [x, kv]
    in_specs = [x_spec, pl.BlockSpec((1, tq, 2 * KV_W), lambda b, t: (b, t, 0))]
    if not banded:
        args.append(past)
        in_specs.append(pl.BlockSpec((1, WINDOW, 2 * KV_W), lambda b, t: (b, 0, 0)))
    args += [g.reshape(1, D), w_q, sinks, w_o]
    in_specs += [_const_spec((1, D)), _const_spec(w_q.shape),
                 pl.BlockSpec(memory_space=pltpu.SMEM), _const_spec(w_o.shape)]
    return pl.pallas_call(
        functools.partial(_swa_body, tq=tq, banded=banded),
        grid=(B, T // tq),
        in_specs=in_specs,
        out_specs=x_spec,
        out_shape=jax.ShapeDtypeStruct((B, T, D), F32),
        scratch_shapes=[pltpu.VMEM((WINDOW + tq, 2 * KV_W), BF16)],
        compiler_params=_params(),
        name="swa_layer",
    )(*args)


def _trunk(x, gla_s, ffn_buf, past, w, *, tm_gla, tm_ffn, tq):
    new_gla, new_buf = [], []
    kv = None
    for layer in range(DEPTH):
        if layer < N_A:
            x, s = _gla_layer(x, gla_s[layer], w["norm_mix"][layer], w["gla_w_in"][layer], w["gla_w_gl"][layer],
                              w["gla_w_g2"][layer], w["gla_b_g2"][layer], w["gla_g_onorm"][layer],
                              w["gla_w_out"][layer], tm=tm_gla)
            new_gla.append(s)
        else:
            j = layer - N_A
            x = _swa_layer(x, kv, past, w["norm_mix"][layer], w["att_w_q"][j], w["att_sinks"][j],
                           w["att_w_o"][j], tq=tq)
        res = _ffn_layer(x, ffn_buf[layer], w["norm_ffn"][layer], w["ffn_w_up"][layer], w["ffn_conv_w"][layer],
                         w["ffn_conv_b"][layer], w["ffn_w_down"][layer], tm=tm_ffn,
                         kv=(w["norm_kv"], w["w_kv"]) if layer == N_A - 1 else None,
                         final=w["norm_final"] if layer == DEPTH - 1 else None)
        x = res[0]
        new_buf.append(res[1])
        if layer == N_A - 1:
            kv = res[2]
    return x, jnp.stack(new_gla), kv, jnp.stack(new_buf)


def kernel(x_prompt, x_sample, state_gla, cache_k, cache_v, state_ffn_conv, norm_mix, norm_ffn, norm_kv, norm_final, gla_w_in, gla_w_g2, gla_b_g2, gla_g_onorm, gla_w_out, w_kv, att_w_q, att_sinks, att_w_o, ffn_w_up, ffn_conv_w, ffn_conv_b, ffn_w_down):
    n_main = 2 * QK_W + 2 * V_W
    w = dict(
        norm_mix=norm_mix, norm_ffn=norm_ffn, norm_kv=norm_kv, norm_final=norm_final,
        gla_w_in=gla_w_in[:, :, :n_main].astype(BF16), gla_w_gl=gla_w_in[:, :, n_main:].astype(BF16),
        gla_w_g2=gla_w_g2.astype(BF16), gla_b_g2=gla_b_g2, gla_g_onorm=gla_g_onorm,
        gla_w_out=gla_w_out.astype(BF16), w_kv=w_kv.astype(BF16), att_w_q=att_w_q.astype(BF16),
        att_sinks=att_sinks, att_w_o=att_w_o.astype(BF16), ffn_w_up=ffn_w_up.astype(BF16),
        ffn_conv_w=ffn_conv_w, ffn_conv_b=ffn_conv_b, ffn_w_down=ffn_w_down.astype(BF16))

    bp = x_prompt.shape[0]
    gla0 = jnp.zeros((N_A, bp, GLA_HEADS, GLA_DK, GLA_DV), F32)
    buf0 = jnp.zeros((DEPTH, bp, CONV_W - 1, 2 * D_FF), F32)
    y_p, gla_p, kv_p, buf_p = _trunk(x_prompt, gla0, buf0, None, w, tm_gla=256, tm_ffn=256, tq=256)
    k_p = kv_p[:, -WINDOW:, :KV_W].reshape(bp, WINDOW, ATT_KV_HEADS, HEAD_DIM)
    v_p = kv_p[:, -WINDOW:, KV_W:].reshape(bp, WINDOW, ATT_KV_HEADS, HEAD_DIM)

    bs, ts = x_sample.shape[:2]
    rows = cache_k.shape[1]
    past = jnp.concatenate([cache_k.reshape(bs, rows, KV_W), cache_v.reshape(bs, rows, KV_W)], axis=-1)
    y_s, gla_s, kv_s, buf_s = _trunk(x_sample, state_gla, state_ffn_conv, past, w, tm_gla=ts, tm_ffn=ts, tq=ts)
    k_s = jnp.concatenate([past[:, :, :KV_W], kv_s[:, :, :KV_W]], axis=1)[:, -rows:]
    v_s = jnp.concatenate([past[:, :, KV_W:], kv_s[:, :, KV_W:]], axis=1)[:, -rows:]
    k_s = k_s.reshape(bs, rows, ATT_KV_HEADS, HEAD_DIM)
    v_s = v_s.reshape(bs, rows, ATT_KV_HEADS, HEAD_DIM)
    return (y_p, y_s, gla_p, k_p, v_p, buf_p, gla_s, k_s, v_s, buf_s)
```

```python
import functools

import jax
import jax.numpy as jnp
from jax import lax
from jax.experimental import pallas as pl
from jax.experimental.pallas import tpu as pltpu

F32 = jnp.float32
BF16 = jnp.bfloat16

D_MODEL = 1024
DEPTH = 4
N_A = DEPTH // 2
CHUNK = 64
WINDOW = 128
GLA_HEADS = 4
GLA_DK = 128
GLA_DV = 256
GLA_RANK = 16
GLA_TAU = 16.0
GLA_BLOCK = 16
GLA_CHUNK = 128
QK_W = GLA_HEADS * GLA_DK
V_W = GLA_HEADS * GLA_DV
ATT_HEADS = 16
ATT_KV_HEADS = 2
ATT_GROUP = ATT_HEADS // ATT_KV_HEADS
HEAD_DIM = 64
KV_W = ATT_KV_HEADS * HEAD_DIM
D_FF = 2816
CONV_W = 3
EPS = 1e-6

VMEM_LIMIT_BYTES = 56 * 1024 * 1024
FFN_SUB = 256


def _rms(x, g):
    ms = jnp.mean(x * x, axis=-1, keepdims=True)
    return x * lax.rsqrt(ms + EPS) * g


def _dot(a, b):
    return jnp.dot(a, b, preferred_element_type=F32)


def _const_spec(shape):
    return pl.BlockSpec(shape, lambda b, t: (0,) * len(shape), pipeline_mode=pl.Buffered(1))


def _params():
    return pltpu.CompilerParams(dimension_semantics=("arbitrary", "arbitrary"),
                                vmem_limit_bytes=VMEM_LIMIT_BYTES)


def _gla_body(x_ref, s0_ref, g_ref, win_ref, wgl_ref, wg2_ref, bg2_ref, gon_ref, wout_ref,
              xo_ref, s_ref, *, tm):
    C = GLA_CHUNK
    L = GLA_BLOCK

    @pl.when(pl.program_id(1) == 0)
    def _init():
        s_ref[...] = s0_ref[...]

    x = x_ref[0]
    h = _rms(x, g_ref[...]).astype(BF16)
    p = _dot(h, win_ref[...])
    gl = _dot(h, wgl_ref[...])
    z = _dot(gl.astype(BF16), wg2_ref[...]) + bg2_ref[...]
    la = jax.nn.log_sigmoid(z) * (1.0 / GLA_TAU)

    rows = max(tm, C)
    if tm < C:
        zpad = lambda a: jnp.concatenate([a, jnp.zeros((C - tm, a.shape[1]), F32)], axis=0)
        p, la = zpad(p), zpad(la)

    ri = lax.broadcasted_iota(jnp.int32, (C, C), 0)
    ci = lax.broadcasted_iota(jnp.int32, (C, C), 1)
    tril = ri >= ci
    tril_f = tril.astype(F32)
    gon = gon_ref[...]

    for c in range(rows // C):
        sl = slice(c * C, (c + 1) * C)
        q = p[sl, 0:QK_W] * (GLA_DK ** -0.5)
        k = p[sl, QK_W:2 * QK_W]
        v = p[sl, 2 * QK_W:2 * QK_W + V_W]
        r = p[sl, 2 * QK_W + V_W:2 * QK_W + 2 * V_W]
        b = jnp.dot(tril_f, la[sl], precision=lax.Precision.HIGHEST, preferred_element_type=F32)
        b_end = b[C - 1:C]
        qt = (q * jnp.exp(b)).astype(BF16)
        kh = (k * jnp.exp(b_end - b)).astype(BF16)
        vb = v.astype(BF16)

        a_rows = []
        for i in range(C // L):
            b_i0 = b[L * i - 1:L * i] if i > 0 else jnp.zeros((1, QK_W), F32)
            qi = (q[L * i:L * (i + 1)] * jnp.exp(b[L * i:L * (i + 1)] - b_i0)).astype(BF16)
            n = L * (i + 1)
            ki = (k[:n] * jnp.exp(b_i0 - b[:n])).astype(BF16)
            if n < C:
                ki = jnp.concatenate([ki, jnp.zeros((C - n, QK_W), BF16)], axis=0)
            a_rows.append([
                lax.dot_general(qi[:, hd * GLA_DK:(hd + 1) * GLA_DK], ki[:, hd * GLA_DK:(hd + 1) * GLA_DK],
                                (((1,), (1,)), ((), ())), preferred_element_type=F32)
                for hd in range(GLA_HEADS)])

        og = []
        for hd in range(GLA_HEADS):
            ks = slice(hd * GLA_DK, (hd + 1) * GLA_DK)
            vs = slice(hd * GLA_DV, (hd + 1) * GLA_DV)
            att = jnp.concatenate([a_rows[i][hd] for i in range(C // L)], axis=0)
            att = jnp.where(tril, att, 0.0).astype(BF16)
            s_old = s_ref[0, hd]
            o = _dot(qt[:, ks], s_old.astype(BF16)) + _dot(att, vb[:, vs])
            ktv = lax.dot_general(kh[:, ks], vb[:, vs], (((0,), (0,)), ((), ())), preferred_element_type=F32)
            g_t = jnp.transpose(jnp.broadcast_to(jnp.exp(b_end[:, ks]), (GLA_DK, GLA_DK)))
            s_ref[0, hd] = s_old * jnp.concatenate([g_t, g_t], axis=1) + ktv
            o = o * lax.rsqrt(jnp.mean(o * o, axis=-1, keepdims=True) + EPS) * gon
            og.append((o * jax.nn.silu(r[:, vs])).astype(BF16))
        y = _dot(jnp.concatenate(og, axis=1), wout_ref[...])
        if tm < C:
            xo_ref[0] = x + y[:tm]
        else:
            xo_ref[0, sl, :] = x[sl] + y


def _gla_layer(x, s0, g, w_in, w_gl, w_g2, b_g2, g_on, w_out, *, tm):
    B, T, D = x.shape
    st_spec = pl.BlockSpec((1, GLA_HEADS, GLA_DK, GLA_DV), lambda b, t: (b, 0, 0, 0))
    return pl.pallas_call(
        functools.partial(_gla_body, tm=tm),
        grid=(B, T // tm),
        in_specs=[
            pl.BlockSpec((1, tm, D), lambda b, t: (b, t, 0)),
            st_spec,
            _const_spec((1, D)),
            _const_spec(w_in.shape),
            _const_spec(w_gl.shape),
            _const_spec(w_g2.shape),
            _const_spec((1, QK_W)),
            _const_spec((1, GLA_DV)),
            _const_spec(w_out.shape),
        ],
        out_specs=[pl.BlockSpec((1, tm, D), lambda b, t: (b, t, 0)), st_spec],
        out_shape=[jax.ShapeDtypeStruct((B, T, D), F32),
                   jax.ShapeDtypeStruct((B, GLA_HEADS, GLA_DK, GLA_DV), F32)],
        compiler_params=_params(),
        name="gla_layer",
    )(x, s0, g.reshape(1, D), w_in, w_gl, w_g2, b_g2.reshape(1, QK_W), g_on.reshape(1, GLA_DV), w_out)


def _ffn_body(*refs, tm, sub, with_kv, with_final):
    x_ref, buf_ref, g_ref, wup_ref, cw_ref, cb_ref, wdn_ref = refs[:7]
    rest = list(refs[7:])
    if with_kv:
        gkv_ref, wkv_ref = rest[:2]
        rest = rest[2:]
    if with_final:
        gfin_ref = rest[0]
        rest = rest[1:]
    xo_ref, tail_ref = rest[:2]
    if with_kv:
        kv_ref = rest[2]

    @pl.when(pl.program_id(1) == 0)
    def _init():
        tail_ref[...] = buf_ref[...]

    row8 = lax.broadcasted_iota(jnp.int32, (8, D_FF), 0)

    def sub_tile(s, hist):
        rs = pl.ds(pl.multiple_of(s * sub, sub), sub)
        x = x_ref[0, rs, :]
        h = _rms(x, g_ref[...]).astype(BF16)
        halves, new_hist = [], []
        for half in range(2):
            cs = slice(half * D_FF, (half + 1) * D_FF)
            u = _dot(h, wup_ref[:, cs])
            r1 = pltpu.roll(u, 1, 0)
            r2 = pltpu.roll(u, 2, 0)
            top1 = jnp.where(row8 == 0, hist[half][1:2], r1[:8])
            top2 = jnp.where(row8 == 0, hist[half][0:1], jnp.where(row8 == 1, hist[half][1:2], r2[:8]))
            if sub > 8:
                r1 = jnp.concatenate([top1, r1[8:]], axis=0)
                r2 = jnp.concatenate([top2, r2[8:]], axis=0)
            else:
                r1, r2 = top1, top2
            new_hist.append(u[sub - 2:sub])
            cw = cw_ref[:, cs]
            halves.append(r2 * cw[0:1] + r1 * cw[1:2] + u * cw[2:3] + cb_ref[:, cs])
        act = (jax.nn.silu(halves[0]) * halves[1]).astype(BF16)
        xo = x + _dot(act, wdn_ref[...])
        if with_kv:
            kv_ref[0, rs, :] = _dot(_rms(xo, gkv_ref[...]).astype(BF16), wkv_ref[...])
        if with_final:
            xo = _rms(xo, gfin_ref[...])
        xo_ref[0, rs, :] = xo
        return tuple(new_hist)

    hist = (tail_ref[0, :, 0:D_FF], tail_ref[0, :, D_FF:2 * D_FF])
    hist = sub_tile(0, hist) if tm == sub else lax.fori_loop(0, tm // sub, sub_tile, hist)
    tail_ref[0, :, 0:D_FF] = hist[0]
    tail_ref[0, :, D_FF:2 * D_FF] = hist[1]


def _ffn_layer(x, buf, g, w_up, conv_w, conv_b, w_down, *, tm, kv=None, final=None):
    B, T, D = x.shape
    tail_spec = pl.BlockSpec((1, CONV_W - 1, 2 * D_FF), lambda b, t: (b, 0, 0))
    x_spec = pl.BlockSpec((1, tm, D), lambda b, t: (b, t, 0))
    args = [x, buf, g.reshape(1, D), w_up, conv_w, conv_b.reshape(1, 2 * D_FF), w_down]
    in_specs = [x_spec, tail_spec, _const_spec((1, D)), _const_spec(w_up.shape), _const_spec(conv_w.shape),
                _const_spec((1, 2 * D_FF)), _const_spec(w_down.shape)]
    out_specs = [x_spec, tail_spec]
    out_shape = [jax.ShapeDtypeStruct((B, T, D), F32), jax.ShapeDtypeStruct((B, CONV_W - 1, 2 * D_FF), F32)]
    if kv is not None:
        g_kv, w_kv = kv
        args += [g_kv.reshape(1, D), w_kv]
        in_specs += [_const_spec((1, D)), _const_spec(w_kv.shape)]
        out_specs.append(pl.BlockSpec((1, tm, 2 * KV_W), lambda b, t: (b, t, 0)))
        out_shape.append(jax.ShapeDtypeStruct((B, T, 2 * KV_W), F32))
    if final is not None:
        args.append(final.reshape(1, D))
        in_specs.append(_const_spec((1, D)))
    return pl.pallas_call(
        functools.partial(_ffn_body, tm=tm, sub=min(tm, FFN_SUB), with_kv=kv is not None, with_final=final is not None),
        grid=(B, T // tm),
        in_specs=in_specs,
        out_specs=out_specs,
        out_shape=out_shape,
        compiler_params=_params(),
        name="conv_ffn",
    )(*args)


def _expand_kv(a):
    rows = a.shape[0]
    lo = lax.broadcasted_iota(jnp.int32, (rows, 2 * HEAD_DIM), 1) < HEAD_DIM
    zero = jnp.zeros((rows, 2 * HEAD_DIM), F32)
    ones_l = jnp.where(lo, 1.0, 0.0)
    ones_r = 1.0 - ones_l

    def halves(c):
        sw = pltpu.roll(c, HEAD_DIM, 1)
        return (jnp.where(lo, c, zero), jnp.where(lo, zero, sw), jnp.where(lo, sw, zero), jnp.where(lo, zero, c))

    k0e, k0o, k1e, k1o = halves(a[:, :KV_W])
    v0e, v0o, v1e, v1o = halves(a[:, KV_W:])
    keys = jnp.concatenate([k0e, k0o, k1e, k1o], axis=1).astype(BF16)
    vals = jnp.concatenate([v0e, ones_l, v0o, ones_r, v1e, ones_l, v1o, ones_r], axis=1).astype(BF16)
    return keys, vals


def _swa_body(*refs, tq, banded):
    if banded:
        x_ref, kv_ref, g_ref, wq_ref, sink_ref, wo_ref, xo_ref, kx, wx = refs
    else:
        x_ref, kv_ref, past_ref, g_ref, wq_ref, sink_ref, wo_ref, xo_ref, kx, wx = refs
    P = WINDOW
    PAIR = 2 * HEAD_DIM
    NP = ATT_GROUP // 2
    t = pl.program_id(1)

    if banded:
        @pl.when(t == 0)
        def _first():
            kx[0:P, :] = jnp.zeros((P, kx.shape[1]), BF16)
            wx[0:P, :] = jnp.zeros((P, wx.shape[1]), BF16)

        @pl.when(t > 0)
        def _carry():
            kx[0:P, :] = kx[tq:tq + P, :]
            wx[0:P, :] = wx[tq:tq + P, :]
    else:
        kx[0:P, :], wx[0:P, :] = _expand_kv(past_ref[0])
    kx[P:P + tq, :], wx[P:P + tq, :] = _expand_kv(kv_ref[0])

    x = x_ref[0]
    h = _rms(x, g_ref[...]).astype(BF16)
    q = (_dot(h, wq_ref[...]) * (HEAD_DIM ** -0.5)).astype(BF16)

    sub = 2 * CHUNK if banded else tq
    lk = sub + P
    lo = lax.broadcasted_iota(jnp.int32, (sub, PAIR), 1) < HEAD_DIM
    nt =(((1,), (1,)), ((), ()))
    outs = []
    for i in range(tq // sub):
        qs = q[i * sub:(i + 1) * sub]
        keys = kx[i * sub:i * sub + lk, :]
        vals = wx[i * sub:i * sub + lk, :]
        if banded:
            ri = lax.broadcasted_iota(jnp.int32, (sub, lk), 0)
            ci = lax.broadcasted_iota(jnp.int32, (sub, lk), 1)
            off = ci - jnp.where(ri >= CHUNK, CHUNK, 0)
            valid = (off >= 0) & (off < WINDOW + CHUNK) & (ci + (t * tq + i * sub - P) >= 0)
        blocks = []
        for n in range(ATT_KV_HEADS):
            qp = jnp.concatenate([qs[:, (n * NP + j) * PAIR:(n * NP + j + 1) * PAIR] for j in range(NP)], axis=0)
            es, sts = [], []
            for par in range(2):
                kn = keys[:, (2 * n + par) * PAIR:(2 * n + par + 1) * PAIR]
                s_all = lax.dot_general(qp, kn, nt, preferred_element_type=F32)
                e_rows, st_rows = [], []
                for j in range(NP):
                    s = s_all[j * sub:(j + 1) * sub]
                    if banded:
                        s = jnp.where(valid, s, -jnp.inf)
                    sk = sink_ref[n * ATT_GROUP + 2 * j + par]
                    m = jnp.maximum(jnp.max(s, axis=-1, keepdims=True), sk)
                    e_rows.append(jnp.exp(s - m).astype(BF16))
                    st_rows.append(jnp.exp(sk - m))
                es.append(jnp.concatenate(e_rows, axis=0))
                sts.append(st_rows)
            xv = (_dot(es[0], vals[:, (2 * n) * 2 * PAIR:(2 * n + 1) * 2 * PAIR])
                  + _dot(es[1], vals[:, (2 * n + 1) * 2 * PAIR:(2 * n + 2) * 2 * PAIR]))
            for j in range(NP):
                xj = xv[j * sub:(j + 1) * sub]
                den = xj[:, PAIR:] + jnp.where(lo, sts[0][j], sts[1][j])
                blocks.append((xj[:, :PAIR] / den).astype(BF16))
        outs.append(jnp.concatenate(blocks, axis=1))
    o = outs[0] if len(outs) == 1 else jnp.concatenate(outs, axis=0)
    xo_ref[0] = x + _dot(o, wo_ref[...])


def _swa_layer(x, kv, past, g, w_q, sinks, w_o, *, tq):
    B, T, D = x.shape
    banded = past is None
    x_spec = pl.BlockSpec((1, tq, D), lambda b, t: (b, t, 0))
    args = [x, kv]
    in_specs = [x_spec, pl.BlockSpec((1, tq, 2 * KV_W), lambda b, t: (b, t, 0))]
    if not banded:
        args.append(past)
        in_specs.append(pl.BlockSpec((1, WINDOW, 2 * KV_W), lambda b, t: (b, 0, 0)))
    args += [g.reshape(1, D), w_q, sinks, w_o]
    in_specs += [_const_spec((1, D)), _const_spec(w_q.shape),
                 pl.BlockSpec(memory_space=pltpu.SMEM), _const_spec(w_o.shape)]
    return pl.pallas_call(
        functools.partial(_swa_body, tq=tq, banded=banded),
        grid=(B, T // tq),
        in_specs=in_specs,
        out_specs=x_spec,
        out_shape=jax.ShapeDtypeStruct((B, T, D), F32),
        scratch_shapes=[pltpu.VMEM((WINDOW + tq, 4 * 2 * HEAD_DIM), BF16),
                        pltpu.VMEM((WINDOW + tq, 4 * 4 * HEAD_DIM), BF16)],
        compiler_params=_params(),
        name="swa_layer",
    )(*args)


def _trunk(x, gla_s, ffn_buf, past, w, *, tm_gla, tm_ffn, tq):
    new_gla, new_buf = [], []
    kv = None
    for layer in range(DEPTH):
        if layer < N_A:
            x, s = _gla_layer(x, gla_s[layer], w["norm_mix"][layer], w["gla_w_in"][layer], w["gla_w_gl"][layer],
                              w["gla_w_g2"][layer], w["gla_b_g2"][layer], w["gla_g_onorm"][layer],
                              w["gla_w_out"][layer], tm=tm_gla)
            new_gla.append(s)
        else:
            j = layer - N_A
            x = _swa_layer(x, kv, past, w["norm_mix"][layer], w["att_w_q"][j], w["att_sinks"][j],
                           w["att_w_o"][j], tq=tq)
        res = _ffn_layer(x, ffn_buf[layer], w["norm_ffn"][layer], w["ffn_w_up"][layer], w["ffn_conv_w"][layer],
                         w["ffn_conv_b"][layer], w["ffn_w_down"][layer], tm=tm_ffn,
                         kv=(w["norm_kv"], w["w_kv"]) if layer == N_A - 1 else None,
                         final=w["norm_final"] if layer == DEPTH - 1 else None)
        x = res[0]
        new_buf.append(res[1])
        if layer == N_A - 1:
            kv = res[2]
    return x, jnp.stack(new_gla), kv, jnp.stack(new_buf)


def kernel(x_prompt, x_sample, state_gla, cache_k, cache_v, state_ffn_conv, norm_mix, norm_ffn, norm_kv, norm_final, gla_w_in, gla_w_g2, gla_b_g2, gla_g_onorm, gla_w_out, w_kv, att_w_q, att_sinks, att_w_o, ffn_w_up, ffn_conv_w, ffn_conv_b, ffn_w_down):
    n_main = 2 * QK_W + 2 * V_W
    w = dict(
        norm_mix=norm_mix, norm_ffn=norm_ffn, norm_kv=norm_kv, norm_final=norm_final,
        gla_w_in=gla_w_in[:, :, :n_main].astype(BF16), gla_w_gl=gla_w_in[:, :, n_main:].astype(BF16),
        gla_w_g2=gla_w_g2.astype(BF16), gla_b_g2=gla_b_g2, gla_g_onorm=gla_g_onorm,
        gla_w_out=gla_w_out.astype(BF16), w_kv=w_kv.astype(BF16), att_w_q=att_w_q.astype(BF16),
        att_sinks=att_sinks, att_w_o=att_w_o.astype(BF16), ffn_w_up=ffn_w_up.astype(BF16),
        ffn_conv_w=ffn_conv_w, ffn_conv_b=ffn_conv_b, ffn_w_down=ffn_w_down.astype(BF16))

    bp = x_prompt.shape[0]
    gla0 = jnp.zeros((N_A, bp, GLA_HEADS, GLA_DK, GLA_DV), F32)
    buf0 = jnp.zeros((DEPTH, bp, CONV_W - 1, 2 * D_FF), F32)
    y_p, gla_p, kv_p, buf_p = _trunk(x_prompt, gla0, buf0, None, w, tm_gla=256, tm_ffn=512, tq=512)
    k_p = kv_p[:, -WINDOW:, :KV_W].reshape(bp, WINDOW, ATT_KV_HEADS, HEAD_DIM)
    v_p = kv_p[:, -WINDOW:, KV_W:].reshape(bp, WINDOW, ATT_KV_HEADS, HEAD_DIM)

    bs, ts = x_sample.shape[:2]
    rows = cache_k.shape[1]
    past = jnp.concatenate([cache_k.reshape(bs, rows, KV_W), cache_v.reshape(bs, rows, KV_W)], axis=-1)
    y_s, gla_s, kv_s, buf_s = _trunk(x_sample, state_gla, state_ffn_conv, past, w, tm_gla=ts, tm_ffn=ts, tq=ts)
    k_s = jnp.concatenate([past[:, :, :KV_W], kv_s[:, :, :KV_W]], axis=1)[:, -rows:]
    v_s = jnp.concatenate([past[:, :, KV_W:], kv_s[:, :, KV_W:]], axis=1)[:, -rows:]
    k_s = k_s.reshape(bs, rows, ATT_KV_HEADS, HEAD_DIM)
    v_s = v_s.reshape(bs, rows, ATT_KV_HEADS, HEAD_DIM)
    return (y_p, y_s, gla_p, k_p, v_p, buf_p, gla_s, k_s, v_s, buf_s)
```

```python
import functools

import jax
import jax.numpy as jnp
from jax import lax
from jax.experimental import pallas as pl
from jax.experimental.pallas import tpu as pltpu

F32 = jnp.float32
BF16 = jnp.bfloat16

D_MODEL = 1024
DEPTH = 4
N_A = DEPTH // 2
CHUNK = 64
WINDOW = 128
GLA_HEADS = 4
GLA_DK = 128
GLA_DV = 256
GLA_RANK = 16
GLA_TAU = 16.0
GLA_BLOCK = 16
GLA_CHUNK = 128
QK_W = GLA_HEADS * GLA_DK
V_W = GLA_HEADS * GLA_DV
ATT_HEADS = 16
ATT_KV_HEADS = 2
ATT_GROUP = ATT_HEADS // ATT_KV_HEADS
HEAD_DIM = 64
KV_W = ATT_KV_HEADS * HEAD_DIM
D_FF = 2816
CONV_W = 3
EPS = 1e-6

VMEM_LIMIT_BYTES = 56 * 1024 * 1024
FFN_ROWS = 256


def _rms(x, g):
    ms = jnp.mean(x * x, axis=-1, keepdims=True)
    return x * lax.rsqrt(ms + EPS) * g


def _dot(a, b):
    return jnp.dot(a, b, preferred_element_type=F32)


def _const_spec(shape):
    return pl.BlockSpec(shape, lambda b, t: (0,) * len(shape), pipeline_mode=pl.Buffered(1))


def _params():
    return pltpu.CompilerParams(dimension_semantics=("arbitrary", "arbitrary"),
                                vmem_limit_bytes=VMEM_LIMIT_BYTES)


SUBLANES = 8


def _cumsum_rows(a):
    n, w = a.shape
    g = n // SUBLANES
    x = a.reshape(g, SUBLANES, w)
    sub = lax.broadcasted_iota(jnp.int32, (g, SUBLANES, w), 1)
    d = 1
    while d < SUBLANES:
        x = x + jnp.where(sub >= d, pltpu.roll(x, d, 1), 0.0)
        d *= 2
    run = jnp.zeros((1, w), F32)
    out = []
    for i in range(g):
        out.append(x[i] + run)
        run = run + x[i, SUBLANES - 1:SUBLANES, :]
    return jnp.concatenate(out, axis=0)

def _gla_body(x_ref, s0_ref, g_ref, win_ref, wgl_ref, wg2_ref, bg2_ref, gon_ref, wout_ref,
              xo_ref, s_ref, *, tm):
    C = GLA_CHUNK
    L = GLA_BLOCK

    @pl.when(pl.program_id(1) == 0)
    def _init():
        s_ref[...] = s0_ref[...]

    x = x_ref[0]
    h = _rms(x, g_ref[...]).astype(BF16)

    gl = _dot(h, wgl_ref[...])
    z = _dot(gl.astype(BF16), wg2_ref[...]) + bg2_ref[...]
    la = (jnp.minimum(z, 0.0) - jnp.log(1.0 + jnp.exp(-jnp.abs(z)))) * (1.0 / GLA_TAU)

    rows = max(tm, C)
    nblk = C // L
    zpad = lambda a: jnp.concatenate([a, jnp.zeros((C - tm, a.shape[1]), F32)], axis=0) if tm < C else a
    la = zpad(la)
    p = zpad(_dot(h, win_ref[...]))

    ri = lax.broadcasted_iota(jnp.int32, (C, C), 0)
    ci = lax.broadcasted_iota(jnp.int32, (C, C), 1)
    tril = ri >= ci
    gon = gon_ref[...]

    decay = []
    for c in range(rows // C):
        b = _cumsum_rows(la[c * C:(c + 1) * C])
        b_end = b[C - 1:C]
        e_q = jnp.exp(b)
        e_k = jnp.exp(b_end - b)
        g_t = [jnp.transpose(jnp.broadcast_to(jnp.exp(b_end[:, hd * GLA_DK:(hd + 1) * GLA_DK]), (GLA_DK, GLA_DK)))
               for hd in range(GLA_HEADS)]
        e_qi, e_ki = [], []
        for i in range(nblk):
            b_i0 = b[L * i - 1:L * i] if i > 0 else jnp.zeros((1, QK_W), F32)
            e_qi.append(jnp.exp(b[L * i:L * (i + 1)] - b_i0))
            e_ki.append(jnp.exp(b_i0 - b[:L * (i + 1)]))
        decay.append((e_q, e_k, g_t, e_qi, e_ki))

    for c in range(rows // C):
        sl = slice(c * C, (c + 1) * C)
        e_q, e_k, g_t, e_qi, e_ki = decay[c]
        q = p[sl, 0:QK_W] * (GLA_DK ** -0.5)
        k = p[sl, QK_W:2 * QK_W]
        v = p[sl, 2 * QK_W:2 * QK_W + V_W]
        r = p[sl, 2 * QK_W + V_W:2 * QK_W + 2 * V_W]
        qt = (q * e_q).astype(BF16)
        kh = (k * e_k).astype(BF16)
        vb = v.astype(BF16)

        a_rows = []
        for i in range(nblk):
            qi = (q[L * i:L * (i + 1)] * e_qi[i]).astype(BF16)
            n = L * (i + 1)
            ki = (k[:n] * e_ki[i]).astype(BF16)
            if n < C:
                ki = jnp.concatenate([ki, jnp.zeros((C - n, QK_W), BF16)], axis=0)
            a_rows.append([
                lax.dot_general(qi[:, hd * GLA_DK:(hd + 1) * GLA_DK], ki[:, hd * GLA_DK:(hd + 1) * GLA_DK],
                                (((1,), (1,)), ((), ())), preferred_element_type=F32)
                for hd in range(GLA_HEADS)])

        og = []
        for hd in range(GLA_HEADS):
            ks = slice(hd * GLA_DK, (hd + 1) * GLA_DK)
            vs = slice(hd * GLA_DV, (hd + 1) * GLA_DV)
            att = jnp.concatenate([a_rows[i][hd] for i in range(nblk)], axis=0)
            att = jnp.where(tril, att, 0.0).astype(BF16)
            s_old = s_ref[0, hd]
            o = _dot(qt[:, ks], s_old.astype(BF16)) + _dot(att, vb[:, vs])
            ktv = lax.dot_general(kh[:, ks], vb[:, vs], (((0,), (0,)), ((), ())), preferred_element_type=F32)
            s_ref[0, hd] = s_old * jnp.concatenate([g_t[hd], g_t[hd]], axis=1) + ktv
            o = o * lax.rsqrt(jnp.mean(o * o, axis=-1, keepdims=True) + EPS) * gon
            og.append((o * jax.nn.silu(r[:, vs])).astype(BF16))
        y = _dot(jnp.concatenate(og, axis=1), wout_ref[...])
        if tm < C:
            xo_ref[0] = x + y[:tm]
        else:
            xo_ref[0, sl, :] = x[sl] + y


def _gla_layer(x, s0, g, w_in, w_gl, w_g2, b_g2, g_on, w_out, *, tm):
    B, T, D = x.shape
    st_spec = pl.BlockSpec((1, GLA_HEADS, GLA_DK, GLA_DV), lambda b, t: (b, 0, 0, 0))
    return pl.pallas_call(
        functools.partial(_gla_body, tm=tm),
        grid=(B, T // tm),
        in_specs=[
            pl.BlockSpec((1, tm, D), lambda b, t: (b, t, 0)),
            st_spec,
            _const_spec((1, D)),
            _const_spec(w_in.shape),
            _const_spec(w_gl.shape),
            _const_spec(w_g2.shape),
            _const_spec((1, QK_W)),
            _const_spec((1, GLA_DV)),
            _const_spec(w_out.shape),
        ],
        out_specs=[pl.BlockSpec((1, tm, D), lambda b, t: (b, t, 0)), st_spec],
        out_shape=[jax.ShapeDtypeStruct((B, T, D), F32),
                   jax.ShapeDtypeStruct((B, GLA_HEADS, GLA_DK, GLA_DV), F32)],
        compiler_params=_params(),
        name="gla_layer",
    )(x, s0, g.reshape(1, D), w_in, w_gl, w_g2, b_g2.reshape(1, QK_W), g_on.reshape(1, GLA_DV), w_out)


def _ffn_body(*refs, tm, seg, with_kv, with_final):
    x_ref, buf_ref, g_ref, wup_ref, cw_ref, cb_ref, wdn_ref = refs[:7]
    rest = list(refs[7:])
    if with_kv:
        gkv_ref, wkv_ref = rest[:2]
        rest = rest[2:]
    if with_final:
        gfin_ref = rest[0]
        rest = rest[1:]
    xo_ref, tail_ref = rest[:2]
    if with_kv:
        kv_ref = rest[2]

    if seg is None:
        @pl.when(pl.program_id(1) == 0)
        def _init():
            tail_ref[...] = buf_ref[...]
        row8 = lax.broadcasted_iota(jnp.int32, (SUBLANES, D_FF), 0)
    else:
        pos = lax.broadcasted_iota(jnp.int32, (tm, D_FF), 0) % seg

    x = x_ref[0]
    h = _rms(x, g_ref[...]).astype(BF16)
    halves = []
    for half in range(2):
        cs = slice(half * D_FF, (half + 1) * D_FF)
        u = _dot(h, wup_ref[:, cs])
        r1 = pltpu.roll(u, 1, 0)
        r2 = pltpu.roll(u, 2, 0)
        if seg is None:
            hist = tail_ref[0, :, cs]
            top1 = jnp.where(row8 == 0, hist[1:2], r1[:SUBLANES])
            top2 = jnp.where(row8 == 0, hist[0:1], jnp.where(row8 == 1, hist[1:2], r2[:SUBLANES]))
            r1 = jnp.concatenate([top1, r1[SUBLANES:]], axis=0)
            r2 = jnp.concatenate([top2, r2[SUBLANES:]], axis=0)
            tail_ref[0, :, cs] = u[tm - 2:tm]
        else:
            hx = buf_ref[0, :, cs]
            r1 = jnp.where(pos == 0, pltpu.roll(hx, tm - 1, 0), r1)
            r2 = jnp.where(pos < 2, hx, r2)
            tail_ref[0, :, cs] = u
        cw = cw_ref[:, cs]
        halves.append(r2 * cw[0:1] + r1 * cw[1:2] + u * cw[2:3] + cb_ref[:, cs])
    act = (jax.nn.silu(halves[0]) * halves[1]).astype(BF16)
    xo = x + _dot(act, wdn_ref[...])
    if with_kv:
        kv_ref[0] = _dot(_rms(xo, gkv_ref[...]).astype(BF16), wkv_ref[...])
    if with_final:
        xo = _rms(xo, gfin_ref[...])
    xo_ref[0] = xo


def _ffn_layer(x, buf, g, w_up, conv_w, conv_b, w_down, *, tm, seg=None, kv=None, final=None):
    B, T, D = x.shape
    x_spec = pl.BlockSpec((1, tm, D), lambda b, t: (b, t, 0))
    if seg is None:
        tail_spec = pl.BlockSpec((1, CONV_W - 1, 2 * D_FF), lambda b, t: (b, 0, 0))
        tail_shape = jax.ShapeDtypeStruct((B, CONV_W - 1, 2 * D_FF), F32)
    else:
        tail_spec = pl.BlockSpec((1, tm, 2 * D_FF), lambda b, t: (b, t, 0))
        tail_shape = jax.ShapeDtypeStruct((B, T, 2 * D_FF), F32)
    args = [x, buf, g.reshape(1, D), w_up, conv_w, conv_b.reshape(1, 2 * D_FF), w_down]
    in_specs = [x_spec, tail_spec, _const_spec((1, D)), _const_spec(w_up.shape), _const_spec(conv_w.shape),
                _const_spec((1, 2 * D_FF)), _const_spec(w_down.shape)]
    out_specs = [x_spec, tail_spec]
    out_shape = [jax.ShapeDtypeStruct((B, T, D), F32), tail_shape]
    if kv is not None:
        g_kv, w_kv = kv
        args += [g_kv.reshape(1, D), w_kv]
        in_specs += [_const_spec((1, D)), _const_spec(w_kv.shape)]
        out_specs.append(pl.BlockSpec((1, tm, 2 * KV_W), lambda b, t: (b, t, 0)))
        out_shape.append(jax.ShapeDtypeStruct((B, T, 2 * KV_W), F32))
    if final is not None:
        args.append(final.reshape(1, D))
        in_specs.append(_const_spec((1, D)))
    return pl.pallas_call(
        functools.partial(_ffn_body, tm=tm, seg=seg, with_kv=kv is not None, with_final=final is not None),
        grid=(B, T // tm),
        in_specs=in_specs,
        out_specs=out_specs,
        out_shape=out_shape,
        compiler_params=_params(),
        name="conv_ffn",
    )(*args)


def _expand_kv(a):
    rows = a.shape[0]
    lo = lax.broadcasted_iota(jnp.int32, (rows, 2 * HEAD_DIM), 1) < HEAD_DIM
    zero = jnp.zeros((rows, 2 * HEAD_DIM), F32)
    ones_l = jnp.where(lo, 1.0, 0.0)
    ones_r = 1.0 - ones_l

    def halves(c):
        sw = pltpu.roll(c, HEAD_DIM, 1)
        return (jnp.where(lo, c, zero), jnp.where(lo, zero, sw), jnp.where(lo, sw, zero), jnp.where(lo, zero, c))

    k0e, k0o, k1e, k1o = halves(a[:, :KV_W])
    v0e, v0o, v1e, v1o = halves(a[:, KV_W:])
    keys = jnp.concatenate([k0e, k0o, k1e, k1o], axis=1).astype(BF16)
    vals = jnp.concatenate([v0e, ones_l, v0o, ones_r, v1e, ones_l, v1o, ones_r], axis=1).astype(BF16)
    return keys, vals


def _swa_body(*refs, tq, banded):
    if banded:
        x_ref, kv_ref, g_ref, wq_ref, sink_ref, wo_ref, xo_ref, kx, wx = refs
    else:
        x_ref, kv_ref, past_ref, g_ref, wq_ref, sink_ref, wo_ref, xo_ref, kx, wx = refs
    P = WINDOW
    PAIR = 2 * HEAD_DIM
    NP = ATT_GROUP // 2
    t = pl.program_id(1)

    if banded:
        @pl.when(t == 0)
        def _first():
            kx[0:P, :] = jnp.zeros((P, kx.shape[1]), BF16)
            wx[0:P, :] = jnp.zeros((P, wx.shape[1]), BF16)

        @pl.when(t > 0)
        def _carry():
            kx[0:P, :] = kx[tq:tq + P, :]
            wx[0:P, :] = wx[tq:tq + P, :]
    else:
        kx[0:P, :], wx[0:P, :] = _expand_kv(past_ref[0])
    kx[P:P + tq, :], wx[P:P + tq, :] = _expand_kv(kv_ref[0])

    x = x_ref[0]
    h = _rms(x, g_ref[...]).astype(BF16)
    q = (_dot(h, wq_ref[...]) * (HEAD_DIM ** -0.5)).astype(BF16)

    sub = 2 * CHUNK if banded else tq
    lk = sub + P
    lo = lax.broadcasted_iota(jnp.int32, (sub, PAIR), 1) < HEAD_DIM
    nt =(((1,), (1,)), ((), ()))
    outs = []
    for i in range(tq // sub):
        qs = q[i * sub:(i + 1) * sub]
        keys = kx[i * sub:i * sub + lk, :]
        vals = wx[i * sub:i * sub + lk, :]
        if banded:
            ri = lax.broadcasted_iota(jnp.int32, (sub, lk), 0)
            ci = lax.broadcasted_iota(jnp.int32, (sub, lk), 1)
            off = ci - jnp.where(ri >= CHUNK, CHUNK, 0)
            valid = (off >= 0) & (off < WINDOW + CHUNK) & (ci + (t * tq + i * sub - P) >= 0)
        blocks = []
        for n in range(ATT_KV_HEADS):
            qp = jnp.concatenate([qs[:, (n * NP + j) * PAIR:(n * NP + j + 1) * PAIR] for j in range(NP)], axis=0)
            es, sts = [], []
            for par in range(2):
                kn = keys[:, (2 * n + par) * PAIR:(2 * n + par + 1) * PAIR]
                s_all = lax.dot_general(qp, kn, nt, preferred_element_type=F32)
                e_rows, st_rows = [], []
                for j in range(NP):
                    s = s_all[j * sub:(j + 1) * sub]
                    if banded:
                        s = jnp.where(valid, s, -jnp.inf)
                    sk = sink_ref[n * ATT_GROUP + 2 * j + par]
                    m = jnp.maximum(jnp.max(s, axis=-1, keepdims=True), sk)
                    e_rows.append(jnp.exp(s - m).astype(BF16))
                    st_rows.append(jnp.exp(sk - m))
                es.append(jnp.concatenate(e_rows, axis=0))
                sts.append(st_rows)
            xv = (_dot(es[0], vals[:, (2 * n) * 2 * PAIR:(2 * n + 1) * 2 * PAIR])
                  + _dot(es[1], vals[:, (2 * n + 1) * 2 * PAIR:(2 * n + 2) * 2 * PAIR]))
            for j in range(NP):
                xj = xv[j * sub:(j + 1) * sub]
                den = xj[:, PAIR:] + jnp.where(lo, sts[0][j], sts[1][j])
                blocks.append((xj[:, :PAIR] / den).astype(BF16))
        outs.append(jnp.concatenate(blocks, axis=1))
    o = outs[0] if len(outs) == 1 else jnp.concatenate(outs, axis=0)
    xo_ref[0] = x + _dot(o, wo_ref[...])


def _swa_layer(x, kv, past, g, w_q, sinks, w_o, *, tq):
    B, T, D = x.shape
    banded = past is None
    x_spec = pl.BlockSpec((1, tq, D), lambda b, t: (b, t, 0))
    args = [x, kv]
    in_specs = [x_spec, pl.BlockSpec((1, tq, 2 * KV_W), lambda b, t: (b, t, 0))]
    if not banded:
        args.append(past)
        in_specs.append(pl.BlockSpec((1, WINDOW, 2 * KV_W), lambda b, t: (b, 0, 0)))
    args += [g.reshape(1, D), w_q, sinks, w_o]
    in_specs += [_const_spec((1, D)), _const_spec(w_q.shape),
                 pl.BlockSpec(memory_space=pltpu.SMEM), _const_spec(w_o.shape)]
    return pl.pallas_call(
        functools.partial(_swa_body, tq=tq, banded=banded),
        grid=(B, T // tq),
        in_specs=in_specs,
        out_specs=x_spec,
        out_shape=jax.ShapeDtypeStruct((B, T, D), F32),
        scratch_shapes=[pltpu.VMEM((WINDOW + tq, 4 * 2 * HEAD_DIM), BF16),
                        pltpu.VMEM((WINDOW + tq, 4 * 4 * HEAD_DIM), BF16)],
        compiler_params=_params(),
        name="swa_layer",
    )(*args)


def _trunk(x, gla_s, ffn_buf, past, w, *, tm_gla, tm_ffn, tq):
    new_gla, new_buf = [], []
    kv = None
    for layer in range(DEPTH):
        if layer < N_A:
            x, s = _gla_layer(x, gla_s[layer], w["norm_mix"][layer], w["gla_w_in"][layer], w["gla_w_gl"][layer],
                              w["gla_w_g2"][layer], w["gla_b_g2"][layer], w["gla_g_onorm"][layer],
                              w["gla_w_out"][layer], tm=tm_gla)
            new_gla.append(s)
        else:
            j = layer - N_A
            x = _swa_layer(x, kv, past, w["norm_mix"][layer], w["att_w_q"][j], w["att_sinks"][j],
                           w["att_w_o"][j], tq=tq)
        ffn_args = (w["norm_ffn"][layer], w["ffn_w_up"][layer], w["ffn_conv_w"][layer], w["ffn_conv_b"][layer],
                    w["ffn_w_down"][layer])
        ffn_opts = dict(kv=(w["norm_kv"], w["w_kv"]) if layer == N_A - 1 else None,
                        final=w["norm_final"] if layer == DEPTH - 1 else None)
        B, T, D = x.shape
        if T >= tm_ffn:
            res = _ffn_layer(x, ffn_buf[layer], *ffn_args, tm=tm_ffn, **ffn_opts)
            tail = res[1]
        else:
            hx = jnp.pad(ffn_buf[layer], ((0, 0), (0, T - (CONV_W - 1)), (0, 0))).reshape(1, B * T, 2 * D_FF)
            res = _ffn_layer(x.reshape(1, B * T, D), hx, *ffn_args, tm=min(B * T, tm_ffn), seg=T, **ffn_opts)
            res = [r.reshape(B, T, r.shape[-1]) for r in res]
            tail = res[1][:, T - (CONV_W - 1):, :]
        x = res[0]
        new_buf.append(tail)
        if layer == N_A - 1:
            kv = res[2]
    return x, jnp.stack(new_gla), kv, jnp.stack(new_buf)


def kernel(x_prompt, x_sample, state_gla, cache_k, cache_v, state_ffn_conv, norm_mix, norm_ffn, norm_kv, norm_final, gla_w_in, gla_w_g2, gla_b_g2, gla_g_onorm, gla_w_out, w_kv, att_w_q, att_sinks, att_w_o, ffn_w_up, ffn_conv_w, ffn_conv_b, ffn_w_down):
    n_main = 2 * QK_W + 2 * V_W
    w = dict(
        norm_mix=norm_mix, norm_ffn=norm_ffn, norm_kv=norm_kv, norm_final=norm_final,
        gla_w_in=gla_w_in[:, :, :n_main].astype(BF16), gla_w_gl=gla_w_in[:, :, n_main:].astype(BF16),
        gla_w_g2=gla_w_g2.astype(BF16), gla_b_g2=gla_b_g2, gla_g_onorm=gla_g_onorm,
        gla_w_out=gla_w_out.astype(BF16), w_kv=w_kv.astype(BF16), att_w_q=att_w_q.astype(BF16),
        att_sinks=att_sinks, att_w_o=att_w_o.astype(BF16), ffn_w_up=ffn_w_up.astype(BF16),
        ffn_conv_w=ffn_conv_w, ffn_conv_b=ffn_conv_b, ffn_w_down=ffn_w_down.astype(BF16))

    bp = x_prompt.shape[0]
    gla0 = jnp.zeros((N_A, bp, GLA_HEADS, GLA_DK, GLA_DV), F32)
    buf0 = jnp.zeros((DEPTH, bp, CONV_W - 1, 2 * D_FF), F32)
    y_p, gla_p, kv_p, buf_p = _trunk(x_prompt, gla0, buf0, None, w, tm_gla=256, tm_ffn=FFN_ROWS, tq=512)
    k_p = kv_p[:, -WINDOW:, :KV_W].reshape(bp, WINDOW, ATT_KV_HEADS, HEAD_DIM)
    v_p = kv_p[:, -WINDOW:, KV_W:].reshape(bp, WINDOW, ATT_KV_HEADS, HEAD_DIM)

    bs, ts = x_sample.shape[:2]
    rows = cache_k.shape[1]
    past = jnp.concatenate([cache_k.reshape(bs, rows, KV_W), cache_v.reshape(bs, rows, KV_W)], axis=-1)
    y_s, gla_s, kv_s, buf_s = _trunk(x_sample, state_gla, state_ffn_conv, past, w, tm_gla=ts, tm_ffn=FFN_ROWS, tq=ts)
    k_s = jnp.concatenate([past[:, :, :KV_W], kv_s[:, :, :KV_W]], axis=1)[:, -rows:]
    v_s = jnp.concatenate([past[:, :, KV_W:], kv_s[:, :, KV_W:]], axis=1)[:, -rows:]
    k_s = k_s.reshape(bs, rows, ATT_KV_HEADS, HEAD_DIM)
    v_s = v_s.reshape(bs, rows, ATT_KV_HEADS, HEAD_DIM)
    return (y_p, y_s, gla_p, k_p, v_p, buf_p, gla_s, k_s, v_s, buf_s)
```

```python
import functools

import jax
import jax.numpy as jnp
from jax import lax
from jax.experimental import pallas as pl
from jax.experimental.pallas import tpu as pltpu

F32 = jnp.float32
BF16 = jnp.bfloat16

D_MODEL = 1024
DEPTH = 4
N_A = DEPTH // 2
CHUNK = 64
WINDOW = 128
GLA_HEADS = 4
GLA_DK = 128
GLA_DV = 256
GLA_RANK = 16
GLA_TAU = 16.0
GLA_BLOCK = 16
GLA_CHUNK = 128
QK_W = GLA_HEADS * GLA_DK
V_W = GLA_HEADS * GLA_DV
ATT_HEADS = 16
ATT_KV_HEADS = 2
ATT_GROUP = ATT_HEADS // ATT_KV_HEADS
HEAD_DIM = 64
KV_W = ATT_KV_HEADS * HEAD_DIM
D_FF = 2816
CONV_W = 3
EPS = 1e-6

VMEM_LIMIT_BYTES = 56 * 1024 * 1024
SHORT_STREAMS = 8
FFN_ROWS = 256


def _rms(x, g):
    ms = jnp.mean(x * x, axis=-1, keepdims=True)
    return x * lax.rsqrt(ms + EPS) * g


def _dot(a, b):
    return jnp.dot(a, b, preferred_element_type=F32)


def _const_spec(shape):
    return pl.BlockSpec(shape, lambda b, t: (0,) * len(shape), pipeline_mode=pl.Buffered(1))


def _params():
    return pltpu.CompilerParams(dimension_semantics=("arbitrary", "arbitrary"),
                                vmem_limit_bytes=VMEM_LIMIT_BYTES)


SUBLANES = 8


def _cumsum_rows(a):
    n, w = a.shape
    g = n // SUBLANES
    x = a.reshape(g, SUBLANES, w)
    sub = lax.broadcasted_iota(jnp.int32, (g, SUBLANES, w), 1)
    d = 1
    while d < SUBLANES:
        x = x + jnp.where(sub >= d, pltpu.roll(x, d, 1), 0.0)
        d *= 2
    run = jnp.zeros((1, w), F32)
    out = []
    for i in range(g):
        out.append(x[i] + run)
        run = run + x[i, SUBLANES - 1:SUBLANES, :]
    return jnp.concatenate(out, axis=0)

def _gla_body(*refs, tm, nb, zero_state):
    if zero_state:
        x_ref, g_ref, win_ref, wgl_ref, wg2_ref, bg2_ref, gon_ref, wout_ref, xo_ref, s_ref = refs
    else:
        x_ref, s0_ref, g_ref, win_ref, wgl_ref, wg2_ref, bg2_ref, gon_ref, wout_ref, xo_ref, s_ref = refs
    C = GLA_CHUNK if nb == 1 else tm
    L = GLA_BLOCK
    nblk = C // L
    units = [(0, c * C) for c in range(tm // C)] if nb == 1 else [(s, s * C) for s in range(nb)]

    @pl.when(pl.program_id(1) == 0)
    def _init():
        s_ref[...] = jnp.zeros(s_ref.shape, F32) if zero_state else s0_ref[...]

    x = x_ref[...].reshape(nb * tm, D_MODEL)
    h = _rms(x, g_ref[...]).astype(BF16)

    gl = _dot(h, wgl_ref[...])
    z = _dot(gl.astype(BF16), wg2_ref[...]) + bg2_ref[...]
    la = (jnp.minimum(z, 0.0) - jnp.log(1.0 + jnp.exp(-jnp.abs(z)))) * (1.0 / GLA_TAU)

    p = _dot(h, win_ref[...])

    ri = lax.broadcasted_iota(jnp.int32, (C, C), 0)
    ci = lax.broadcasted_iota(jnp.int32, (C, C), 1)
    tril = ri >= ci
    gon = gon_ref[...]

    decay = []
    for _, r0 in units:
        b = _cumsum_rows(la[r0:r0 + C])
        b_end = b[C - 1:C]
        e_q = jnp.exp(b)
        e_k = jnp.exp(b_end - b)
        g_t = [jnp.transpose(jnp.broadcast_to(jnp.exp(b_end[:, hd * GLA_DK:(hd + 1) * GLA_DK]), (GLA_DK, GLA_DK)))
               for hd in range(GLA_HEADS)]
        e_qi, e_ki = [], []
        for i in range(nblk):
            b_i0 = b[L * i - 1:L * i] if i > 0 else jnp.zeros((1, QK_W), F32)
            e_qi.append(jnp.exp(b[L * i:L * (i + 1)] - b_i0))
            e_ki.append(jnp.exp(b_i0 - b[:L * (i + 1)]))
        decay.append((e_q, e_k, g_t, e_qi, e_ki))

    gated = []
    for (slot, r0), (e_q, e_k, g_t, e_qi, e_ki) in zip(units, decay):
        sl = slice(r0, r0 + C)
        q = p[sl, 0:QK_W] * (GLA_DK ** -0.5)
        k = p[sl, QK_W:2 * QK_W]
        v = p[sl, 2 * QK_W:2 * QK_W + V_W]
        r = p[sl, 2 * QK_W + V_W:2 * QK_W + 2 * V_W]
        qt = (q * e_q).astype(BF16)
        kh = (k * e_k).astype(BF16)
        vb = v.astype(BF16)

        a_rows = []
        for i in range(nblk):
            qi = (q[L * i:L * (i + 1)] * e_qi[i]).astype(BF16)
            n = L * (i + 1)
            ki = (k[:n] * e_ki[i]).astype(BF16)
            if n < C:
                ki = jnp.concatenate([ki, jnp.zeros((C - n, QK_W), BF16)], axis=0)
            a_rows.append([
                lax.dot_general(qi[:, hd * GLA_DK:(hd + 1) * GLA_DK], ki[:, hd * GLA_DK:(hd + 1) * GLA_DK],
                                (((1,), (1,)), ((), ())), preferred_element_type=F32)
                for hd in range(GLA_HEADS)])

        og = []
        for hd in range(GLA_HEADS):
            ks = slice(hd * GLA_DK, (hd + 1) * GLA_DK)
            vs = slice(hd * GLA_DV, (hd + 1) * GLA_DV)
            att = jnp.concatenate([a_rows[i][hd] for i in range(nblk)], axis=0)
            att = jnp.where(tril, att, 0.0).astype(BF16)
            s_old = s_ref[slot, hd]
            o = _dot(qt[:, ks], s_old.astype(BF16)) + _dot(att, vb[:, vs])
            ktv = lax.dot_general(kh[:, ks], vb[:, vs], (((0,), (0,)), ((), ())), preferred_element_type=F32)
            s_ref[slot, hd] = s_old * jnp.concatenate([g_t[hd], g_t[hd]], axis=1) + ktv
            o = o * lax.rsqrt(jnp.mean(o * o, axis=-1, keepdims=True) + EPS) * gon
            og.append((o * jax.nn.silu(r[:, vs])).astype(BF16))
        og = jnp.concatenate(og, axis=1)
        if nb == 1:
            xo_ref[0, sl, :] = x[sl] + _dot(og, wout_ref[...])
        else:
            gated.append(og)
    if nb > 1:
        y = x + _dot(jnp.concatenate(gated, axis=0), wout_ref[...])
        xo_ref[...] = y.reshape(nb, tm, D_MODEL)


def _gla_layer(x, s0, layer, g, w_in, w_gl, w_g2, b_g2, g_on, w_out, *, tm, nb):
    B, T, D = x.shape
    x_spec = pl.BlockSpec((nb, tm, D), lambda b, t: (b, t, 0))
    st_block = (nb, GLA_HEADS, GLA_DK, GLA_DV)
    args, in_specs = [x], [x_spec]
    if s0 is not None:
        args.append(s0)
        in_specs.append(pl.BlockSpec((None,) + st_block, lambda b, t: (layer, b, 0, 0, 0)))
    args += [g.reshape(1, D), w_in, w_gl, w_g2, b_g2.reshape(1, QK_W), g_on.reshape(1, GLA_DV), w_out]
    in_specs += [_const_spec((1, D)), _const_spec(w_in.shape), _const_spec(w_gl.shape), _const_spec(w_g2.shape),
                 _const_spec((1, QK_W)), _const_spec((1, GLA_DV)), _const_spec(w_out.shape)]
    return pl.pallas_call(
        functools.partial(_gla_body, tm=tm, nb=nb, zero_state=s0 is None),
        grid=(B // nb, T // tm),
        in_specs=in_specs,
        out_specs=[x_spec, pl.BlockSpec(st_block, lambda b, t: (b, 0, 0, 0))],
        out_shape=[jax.ShapeDtypeStruct((B, T, D), F32),
                   jax.ShapeDtypeStruct((B, GLA_HEADS, GLA_DK, GLA_DV), F32)],
        compiler_params=_params(),
        name="gla_layer",
    )(*args)


def _ffn_body(*refs, tm, seg, with_kv, with_final):
    x_ref, buf_ref, g_ref, wup_ref, cw_ref, cb_ref, wdn_ref = refs[:7]
    rest = list(refs[7:])
    if with_kv:
        gkv_ref, wkv_ref = rest[:2]
        rest = rest[2:]
    if with_final:
        gfin_ref = rest[0]
        rest = rest[1:]
    xo_ref, tail_ref = rest[:2]
    if with_kv:
        kv_ref = rest[2]

    if seg is None:
        @pl.when(pl.program_id(1) == 0)
        def _init():
            tail_ref[...] = buf_ref[...]
        row8 = lax.broadcasted_iota(jnp.int32, (SUBLANES, D_FF), 0)
    else:
        pos = lax.broadcasted_iota(jnp.int32, (tm, D_FF), 0) % seg

    x = x_ref[0]
    h = _rms(x, g_ref[...]).astype(BF16)
    halves = []
    for half in range(2):
        cs = slice(half * D_FF, (half + 1) * D_FF)
        u = _dot(h, wup_ref[:, cs])
        r1 = pltpu.roll(u, 1, 0)
        r2 = pltpu.roll(u, 2, 0)
        if seg is None:
            hist = tail_ref[0, :, cs]
            top1 = jnp.where(row8 == 0, hist[1:2], r1[:SUBLANES])
            top2 = jnp.where(row8 == 0, hist[0:1], jnp.where(row8 == 1, hist[1:2], r2[:SUBLANES]))
            r1 = jnp.concatenate([top1, r1[SUBLANES:]], axis=0)
            r2 = jnp.concatenate([top2, r2[SUBLANES:]], axis=0)
            tail_ref[0, :, cs] = u[tm - 2:tm]
        else:
            hx = buf_ref[0, :, cs]
            r1 = jnp.where(pos == 0, pltpu.roll(hx, tm - 1, 0), r1)
            r2 = jnp.where(pos < 2, hx, r2)
            tail_ref[0, :, cs] = u
        cw = cw_ref[:, cs]
        halves.append(r2 * cw[0:1] + r1 * cw[1:2] + u * cw[2:3] + cb_ref[:, cs])
    act = (jax.nn.silu(halves[0]) * halves[1]).astype(BF16)
    xo = x + _dot(act, wdn_ref[...])
    if with_kv:
        kv_ref[0] = _dot(_rms(xo, gkv_ref[...]).astype(BF16), wkv_ref[...])
    if with_final:
        xo = _rms(xo, gfin_ref[...])
    xo_ref[0] = xo


def _ffn_layer(x, buf, g, w_up, conv_w, conv_b, w_down, *, tm, seg=None, kv=None, final=None):
    B, T, D = x.shape
    x_spec = pl.BlockSpec((1, tm, D), lambda b, t: (b, t, 0))
    if seg is None:
        tail_spec = pl.BlockSpec((1, CONV_W - 1, 2 * D_FF), lambda b, t: (b, 0, 0))
        tail_shape = jax.ShapeDtypeStruct((B, CONV_W - 1, 2 * D_FF), F32)
    else:
        tail_spec = pl.BlockSpec((1, tm, 2 * D_FF), lambda b, t: (b, t, 0))
        tail_shape = jax.ShapeDtypeStruct((B, T, 2 * D_FF), F32)
    args = [x, buf, g.reshape(1, D), w_up, conv_w, conv_b.reshape(1, 2 * D_FF), w_down]
    in_specs = [x_spec, tail_spec, _const_spec((1, D)), _const_spec(w_up.shape), _const_spec(conv_w.shape),
                _const_spec((1, 2 * D_FF)), _const_spec(w_down.shape)]
    out_specs = [x_spec, tail_spec]
    out_shape = [jax.ShapeDtypeStruct((B, T, D), F32), tail_shape]
    if kv is not None:
        g_kv, w_kv = kv
        args += [g_kv.reshape(1, D), w_kv]
        in_specs += [_const_spec((1, D)), _const_spec(w_kv.shape)]
        out_specs.append(pl.BlockSpec((1, tm, 2 * KV_W), lambda b, t: (b, t, 0)))
        out_shape.append(jax.ShapeDtypeStruct((B, T, 2 * KV_W), F32))
    if final is not None:
        args.append(final.reshape(1, D))
        in_specs.append(_const_spec((1, D)))
    return pl.pallas_call(
        functools.partial(_ffn_body, tm=tm, seg=seg, with_kv=kv is not None, with_final=final is not None),
        grid=(B, T // tm),
        in_specs=in_specs,
        out_specs=out_specs,
        out_shape=out_shape,
        compiler_params=_params(),
        name="conv_ffn",
    )(*args)


def _expand_kv(a):
    rows = a.shape[0]
    lo = lax.broadcasted_iota(jnp.int32, (rows, 2 * HEAD_DIM), 1) < HEAD_DIM
    zero = jnp.zeros((rows, 2 * HEAD_DIM), F32)
    ones_l = jnp.where(lo, 1.0, 0.0)
    ones_r = 1.0 - ones_l

    def halves(c):
        sw = pltpu.roll(c, HEAD_DIM, 1)
        return (jnp.where(lo, c, zero), jnp.where(lo, zero, sw), jnp.where(lo, sw, zero), jnp.where(lo, zero, c))

    k0e, k0o, k1e, k1o = halves(a[:, :KV_W])
    v0e, v0o, v1e, v1o = halves(a[:, KV_W:])
    keys = jnp.concatenate([k0e, k0o, k1e, k1o], axis=1).astype(BF16)
    vals = jnp.concatenate([v0e, ones_l, v0o, ones_r, v1e, ones_l, v1o, ones_r], axis=1).astype(BF16)
    return keys, vals


def _swa_body(*refs, tq, nb, banded):
    if banded:
        x_ref, kv_ref, g_ref, wq_ref, sink_ref, wo_ref, xo_ref, kx, wx = refs
    else:
        x_ref, kv_ref, past_ref, g_ref, wq_ref, sink_ref, wo_ref, xo_ref = refs
    P = WINDOW
    PAIR = 2 * HEAD_DIM
    NP = ATT_GROUP // 2
    t = pl.program_id(1)

    if banded:
        @pl.when(t == 0)
        def _first():
            kx[0:P, :] = jnp.zeros((P, kx.shape[1]), BF16)
            wx[0:P, :] = jnp.zeros((P, wx.shape[1]), BF16)

        @pl.when(t > 0)
        def _carry():
            kx[0:P, :] = kx[tq:tq + P, :]
            wx[0:P, :] = wx[tq:tq + P, :]

        kx[P:P + tq, :], wx[P:P + tq, :] = _expand_kv(kv_ref[0])
    else:
        past_k, past_v = _expand_kv(past_ref[...].reshape(nb * P, 2 * KV_W))
        new_k, new_v = _expand_kv(kv_ref[...].reshape(nb * tq, 2 * KV_W))

    x = x_ref[...].reshape(nb * tq, D_MODEL)
    h = _rms(x, g_ref[...]).astype(BF16)
    q = (_dot(h, wq_ref[...]) * (HEAD_DIM ** -0.5)).astype(BF16)

    sub = 2 * CHUNK if banded else tq
    lk = sub + P
    lo = lax.broadcasted_iota(jnp.int32, (sub, PAIR), 1) < HEAD_DIM
    nt = (((1,), (1,)), ((), ()))
    outs = []
    for i in range(nb * tq // sub):
        qs = q[i * sub:(i + 1) * sub]
        if banded:
            keys = kx[i * sub:i * sub + lk, :]
            vals = wx[i * sub:i * sub + lk, :]
            ri = lax.broadcasted_iota(jnp.int32, (sub, lk), 0)
            ci = lax.broadcasted_iota(jnp.int32, (sub, lk), 1)
            off = ci - jnp.where(ri >= CHUNK, CHUNK, 0)
            valid = (off >= 0) & (off < WINDOW + CHUNK) & (ci + (t * tq + i * sub - P) >= 0)
        else:
            keys = jnp.concatenate([past_k[i * P:(i + 1) * P], new_k[i * sub:(i + 1) * sub]], axis=0)
            vals = jnp.concatenate([past_v[i * P:(i + 1) * P], new_v[i * sub:(i + 1) * sub]], axis=0)
        blocks = []
        for n in range(ATT_KV_HEADS):
            qp = jnp.concatenate([qs[:, (n * NP + j) * PAIR:(n * NP + j + 1) * PAIR] for j in range(NP)], axis=0)
            es, sts = [], []
            for par in range(2):
                kn = keys[:, (2 * n + par) * PAIR:(2 * n + par + 1) * PAIR]
                s_all = lax.dot_general(qp, kn, nt, preferred_element_type=F32)
                e_rows, st_rows = [], []
                for j in range(NP):
                    s = s_all[j * sub:(j + 1) * sub]
                    if banded:
                        s = jnp.where(valid, s, -jnp.inf)
                    sk = sink_ref[n * ATT_GROUP + 2 * j + par]
                    m = jnp.maximum(jnp.max(s, axis=-1, keepdims=True), sk)
                    e_rows.append(jnp.exp(s - m).astype(BF16))
                    st_rows.append(jnp.exp(sk - m))
                es.append(jnp.concatenate(e_rows, axis=0))
                sts.append(st_rows)
            xv = (_dot(es[0], vals[:, (2 * n) * 2 * PAIR:(2 * n + 1) * 2 * PAIR])
                  + _dot(es[1], vals[:, (2 * n + 1) * 2 * PAIR:(2 * n + 2) * 2 * PAIR]))
            for j in range(NP):
                xj = xv[j * sub:(j + 1) * sub]
                den = xj[:, PAIR:] + jnp.where(lo, sts[0][j], sts[1][j])
                blocks.append((xj[:, :PAIR] / den).astype(BF16))
        outs.append(jnp.concatenate(blocks, axis=1))
    o = outs[0] if len(outs) == 1 else jnp.concatenate(outs, axis=0)
    xo_ref[...] = (x + _dot(o, wo_ref[...])).reshape(nb, tq, D_MODEL)


def _swa_layer(x, kv, past, g, w_q, sinks, w_o, *, tq, nb):
    B, T, D = x.shape
    banded = past is None
    x_spec = pl.BlockSpec((nb, tq, D), lambda b, t: (b, t, 0))
    args = [x, kv]
    in_specs = [x_spec, pl.BlockSpec((nb, tq, 2 * KV_W), lambda b, t: (b, t, 0))]
    scratch = []
    if banded:
        scratch = [pltpu.VMEM((WINDOW + tq, 4 * 2 * HEAD_DIM), BF16), pltpu.VMEM((WINDOW + tq, 4 * 4 * HEAD_DIM), BF16)]
    else:
        args.append(past)
        in_specs.append(pl.BlockSpec((nb, WINDOW, 2 * KV_W), lambda b, t: (b, 0, 0)))
    args += [g.reshape(1, D), w_q, sinks, w_o]
    in_specs += [_const_spec((1, D)), _const_spec(w_q.shape),
                 pl.BlockSpec(memory_space=pltpu.SMEM), _const_spec(w_o.shape)]
    return pl.pallas_call(
        functools.partial(_swa_body, tq=tq, nb=nb, banded=banded),
        grid=(B // nb, T // tq),
        in_specs=in_specs,
        out_specs=x_spec,
        out_shape=jax.ShapeDtypeStruct((B, T, D), F32),
        scratch_shapes=scratch,
        compiler_params=_params(),
        name="swa_layer",
    )(*args)


def _trunk(x, gla_s, ffn_buf, past, w, *, tm_gla, nb_streams, tm_ffn, tq):
    new_gla, new_buf = [], []
    kv = None
    for layer in range(DEPTH):
        if layer < N_A:
            x, s = _gla_layer(x, gla_s, layer, w["norm_mix"][layer], w["gla_w_in"][layer], w["gla_w_gl"][layer],
                              w["gla_w_g2"][layer], w["gla_b_g2"][layer], w["gla_g_onorm"][layer],
                              w["gla_w_out"][layer], tm=tm_gla, nb=nb_streams)
            new_gla.append(s)
        else:
            j = layer - N_A
            x = _swa_layer(x, kv, past, w["norm_mix"][layer], w["att_w_q"][j], w["att_sinks"][j],
                           w["att_w_o"][j], tq=tq, nb=nb_streams)
        ffn_args = (w["norm_ffn"][layer], w["ffn_w_up"][layer], w["ffn_conv_w"][layer], w["ffn_conv_b"][layer],
                    w["ffn_w_down"][layer])
        ffn_opts = dict(kv=(w["norm_kv"], w["w_kv"]) if layer == N_A - 1 else None,
                        final=w["norm_final"] if layer == DEPTH - 1 else None)
        B, T, D = x.shape
        if T >= tm_ffn:
            res = _ffn_layer(x, ffn_buf[layer], *ffn_args, tm=tm_ffn, **ffn_opts)
            tail = res[1]
        else:
            hx = jnp.pad(ffn_buf[layer], ((0, 0), (0, T - (CONV_W - 1)), (0, 0))).reshape(1, B * T, 2 * D_FF)
            res = _ffn_layer(x.reshape(1, B * T, D), hx, *ffn_args, tm=min(B * T, tm_ffn), seg=T, **ffn_opts)
            res = [r.reshape(B, T, r.shape[-1]) for r in res]
            tail = res[1][:, T - (CONV_W - 1):, :]
        x = res[0]
        new_buf.append(tail)
        if layer == N_A - 1:
            kv = res[2]
    return x, jnp.stack(new_gla), kv, jnp.stack(new_buf)


def kernel(x_prompt, x_sample, state_gla, cache_k, cache_v, state_ffn_conv, norm_mix, norm_ffn, norm_kv, norm_final, gla_w_in, gla_w_g2, gla_b_g2, gla_g_onorm, gla_w_out, w_kv, att_w_q, att_sinks, att_w_o, ffn_w_up, ffn_conv_w, ffn_conv_b, ffn_w_down):
    n_main = 2 * QK_W + 2 * V_W
    w = dict(
        norm_mix=norm_mix, norm_ffn=norm_ffn, norm_kv=norm_kv, norm_final=norm_final,
        gla_w_in=gla_w_in[:, :, :n_main].astype(BF16), gla_w_gl=gla_w_in[:, :, n_main:].astype(BF16),
        gla_w_g2=gla_w_g2.astype(BF16), gla_b_g2=gla_b_g2, gla_g_onorm=gla_g_onorm,
        gla_w_out=gla_w_out.astype(BF16), w_kv=w_kv.astype(BF16), att_w_q=att_w_q.astype(BF16),
        att_sinks=att_sinks, att_w_o=att_w_o.astype(BF16), ffn_w_up=ffn_w_up.astype(BF16),
        ffn_conv_w=ffn_conv_w, ffn_conv_b=ffn_conv_b, ffn_w_down=ffn_w_down.astype(BF16))

    bp = x_prompt.shape[0]
    buf0 = jnp.zeros((DEPTH, bp, CONV_W - 1, 2 * D_FF), F32)
    y_p, gla_p, kv_p, buf_p = _trunk(x_prompt, None, buf0, None, w, tm_gla=512, nb_streams=1, tm_ffn=FFN_ROWS, tq=512)
    k_p = kv_p[:, -WINDOW:, :KV_W].reshape(bp, WINDOW, ATT_KV_HEADS, HEAD_DIM)
    v_p = kv_p[:, -WINDOW:, KV_W:].reshape(bp, WINDOW, ATT_KV_HEADS, HEAD_DIM)

    bs, ts = x_sample.shape[:2]
    rows = cache_k.shape[1]
    past = jnp.concatenate([cache_k.reshape(bs, rows, KV_W), cache_v.reshape(bs, rows, KV_W)], axis=-1)
    y_s, gla_s, kv_s, buf_s = _trunk(x_sample, state_gla, state_ffn_conv, past, w, tm_gla=ts, nb_streams=SHORT_STREAMS,
                                     tm_ffn=FFN_ROWS, tq=ts)
    k_s = jnp.concatenate([past[:, :, :KV_W], kv_s[:, :, :KV_W]], axis=1)[:, -rows:]
    v_s = jnp.concatenate([past[:, :, KV_W:], kv_s[:, :, KV_W:]], axis=1)[:, -rows:]
    k_s = k_s.reshape(bs, rows, ATT_KV_HEADS, HEAD_DIM)
    v_s = v_s.reshape(bs, rows, ATT_KV_HEADS, HEAD_DIM)
    return (y_p, y_s, gla_p, k_p, v_p, buf_p, gla_s, k_s, v_s, buf_s)
```

```python
import functools

import jax
import jax.numpy as jnp
from jax import lax
from jax.experimental import pallas as pl
from jax.experimental.pallas import tpu as pltpu

F32 = jnp.float32
BF16 = jnp.bfloat16

D_MODEL = 1024
DEPTH = 4
N_A = DEPTH // 2
CHUNK = 64
WINDOW = 128
GLA_HEADS = 4
GLA_DK = 128
GLA_DV = 256
GLA_RANK = 16
GLA_TAU = 16.0
GLA_BLOCK = 16
GLA_CHUNK = 128
QK_W = GLA_HEADS * GLA_DK
V_W = GLA_HEADS * GLA_DV
ATT_HEADS = 16
ATT_KV_HEADS = 2
ATT_GROUP = ATT_HEADS // ATT_KV_HEADS
HEAD_DIM = 64
KV_W = ATT_KV_HEADS * HEAD_DIM
D_FF = 2816
CONV_W = 3
EPS = 1e-6

VMEM_LIMIT_BYTES = 56 * 1024 * 1024
SHORT_STREAMS = 8
FFN_FOLD_ROWS = 256
FFN_ROWS = 512


def _rms(x, g):
    ms = jnp.mean(x * x, axis=-1, keepdims=True)
    return x * lax.rsqrt(ms + EPS) * g


def _dot(a, b):
    return jnp.dot(a, b, preferred_element_type=F32)


def _const_spec(shape):
    return pl.BlockSpec(shape, lambda b, t: (0,) * len(shape), pipeline_mode=pl.Buffered(1))


def _params():
    return pltpu.CompilerParams(dimension_semantics=("arbitrary", "arbitrary"),
                                vmem_limit_bytes=VMEM_LIMIT_BYTES)


SUBLANES = 8


def _cumsum_rows(a):
    n, w = a.shape
    g = n // SUBLANES
    x = a.reshape(g, SUBLANES, w)
    sub = lax.broadcasted_iota(jnp.int32, (g, SUBLANES, w), 1)
    d = 1
    while d < SUBLANES:
        x = x + jnp.where(sub >= d, pltpu.roll(x, d, 1), 0.0)
        d *= 2
    run = jnp.zeros((1, w), F32)
    out = []
    for i in range(g):
        out.append(x[i] + run)
        run = run + x[i, SUBLANES - 1:SUBLANES, :]
    return jnp.concatenate(out, axis=0)

def _gla_body(*refs, tm, nb, zero_state):
    if zero_state:
        x_ref, g_ref, win_ref, wgl_ref, wg2_ref, bg2_ref, gon_ref, wout_ref, xo_ref, s_ref = refs
    else:
        x_ref, s0_ref, g_ref, win_ref, wgl_ref, wg2_ref, bg2_ref, gon_ref, wout_ref, xo_ref, s_ref = refs
    C = GLA_CHUNK if nb == 1 else tm
    L = GLA_BLOCK
    nblk = C // L
    units = [(0, c * C) for c in range(tm // C)] if nb == 1 else [(s, s * C) for s in range(nb)]

    @pl.when(pl.program_id(1) == 0)
    def _init():
        s_ref[...] = jnp.zeros(s_ref.shape, F32) if zero_state else s0_ref[...]

    x = x_ref[...].reshape(nb * tm, D_MODEL)
    h = _rms(x, g_ref[...]).astype(BF16)

    gl = _dot(h, wgl_ref[...])
    z = _dot(gl.astype(BF16), wg2_ref[...]) + bg2_ref[...]
    la = (jnp.minimum(z, 0.0) - jnp.log(1.0 + jnp.exp(-jnp.abs(z)))) * (1.0 / GLA_TAU)

    p = _dot(h, win_ref[...])

    ri = lax.broadcasted_iota(jnp.int32, (C, C), 0)
    ci = lax.broadcasted_iota(jnp.int32, (C, C), 1)
    tril = ri >= ci
    gon = gon_ref[...]

    decay = []
    for _, r0 in units:
        b = _cumsum_rows(la[r0:r0 + C])
        b_end = b[C - 1:C]
        e_q = jnp.exp(b)
        e_k = jnp.exp(b_end - b)
        g_t = [jnp.transpose(jnp.broadcast_to(jnp.exp(b_end[:, hd * GLA_DK:(hd + 1) * GLA_DK]), (GLA_DK, GLA_DK)))
               for hd in range(GLA_HEADS)]
        e_qi, e_ki = [], []
        for i in range(nblk):
            b_i0 = b[L * i - 1:L * i] if i > 0 else jnp.zeros((1, QK_W), F32)
            e_qi.append(jnp.exp(b[L * i:L * (i + 1)] - b_i0))
            e_ki.append(jnp.exp(b_i0 - b[:L * (i + 1)]))
        decay.append((e_q, e_k, g_t, e_qi, e_ki))

    gated = []
    for (slot, r0), (e_q, e_k, g_t, e_qi, e_ki) in zip(units, decay):
        sl = slice(r0, r0 + C)
        q = p[sl, 0:QK_W] * (GLA_DK ** -0.5)
        k = p[sl, QK_W:2 * QK_W]
        v = p[sl, 2 * QK_W:2 * QK_W + V_W]
        r = p[sl, 2 * QK_W + V_W:2 * QK_W + 2 * V_W]
        qt = (q * e_q).astype(BF16)
        kh = (k * e_k).astype(BF16)
        vb = v.astype(BF16)

        a_rows = []
        for i in range(nblk):
            qi = (q[L * i:L * (i + 1)] * e_qi[i]).astype(BF16)
            n = L * (i + 1)
            ki = (k[:n] * e_ki[i]).astype(BF16)
            if n < C:
                ki = jnp.concatenate([ki, jnp.zeros((C - n, QK_W), BF16)], axis=0)
            a_rows.append([
                lax.dot_general(qi[:, hd * GLA_DK:(hd + 1) * GLA_DK], ki[:, hd * GLA_DK:(hd + 1) * GLA_DK],
                                (((1,), (1,)), ((), ())), preferred_element_type=F32)
                for hd in range(GLA_HEADS)])

        og = []
        for hd in range(GLA_HEADS):
            ks = slice(hd * GLA_DK, (hd + 1) * GLA_DK)
            vs = slice(hd * GLA_DV, (hd + 1) * GLA_DV)
            att = jnp.concatenate([a_rows[i][hd] for i in range(nblk)], axis=0)
            att = jnp.where(tril, att, 0.0).astype(BF16)
            s_old = s_ref[slot, hd]
            o = _dot(qt[:, ks], s_old.astype(BF16)) + _dot(att, vb[:, vs])
            ktv = lax.dot_general(kh[:, ks], vb[:, vs], (((0,), (0,)), ((), ())), preferred_element_type=F32)
            s_ref[slot, hd] = s_old * jnp.concatenate([g_t[hd], g_t[hd]], axis=1) + ktv
            o = o * lax.rsqrt(jnp.mean(o * o, axis=-1, keepdims=True) + EPS) * gon
            og.append((o * jax.nn.silu(r[:, vs])).astype(BF16))
        og = jnp.concatenate(og, axis=1)
        if nb == 1:
            xo_ref[0, sl, :] = x[sl] + _dot(og, wout_ref[...])
        else:
            gated.append(og)
    if nb > 1:
        y = x + _dot(jnp.concatenate(gated, axis=0), wout_ref[...])
        xo_ref[...] = y.reshape(nb, tm, D_MODEL)


def _gla_layer(x, s0, layer, g, w_in, w_gl, w_g2, b_g2, g_on, w_out, *, tm, nb):
    B, T, D = x.shape
    x_spec = pl.BlockSpec((nb, tm, D), lambda b, t: (b, t, 0))
    st_block = (nb, GLA_HEADS, GLA_DK, GLA_DV)
    args, in_specs = [x], [x_spec]
    if s0 is not None:
        args.append(s0)
        in_specs.append(pl.BlockSpec((None,) + st_block, lambda b, t: (layer, b, 0, 0, 0)))
    args += [g.reshape(1, D), w_in, w_gl, w_g2, b_g2.reshape(1, QK_W), g_on.reshape(1, GLA_DV), w_out]
    in_specs += [_const_spec((1, D)), _const_spec(w_in.shape), _const_spec(w_gl.shape), _const_spec(w_g2.shape),
                 _const_spec((1, QK_W)), _const_spec((1, GLA_DV)), _const_spec(w_out.shape)]
    return pl.pallas_call(
        functools.partial(_gla_body, tm=tm, nb=nb, zero_state=s0 is None),
        grid=(B // nb, T // tm),
        in_specs=in_specs,
        out_specs=[x_spec, pl.BlockSpec(st_block, lambda b, t: (b, 0, 0, 0))],
        out_shape=[jax.ShapeDtypeStruct((B, T, D), F32),
                   jax.ShapeDtypeStruct((B, GLA_HEADS, GLA_DK, GLA_DV), F32)],
        compiler_params=_params(),
        name="gla_layer",
    )(*args)


def _ffn_body(*refs, tm, seg, with_final):
    x_ref, buf_ref, g_ref, wup_ref, cw_ref, cb_ref, wdn_ref = refs[:7]
    rest = list(refs[7:])
    if with_final:
        gfin_ref = rest.pop(0)
    xo_ref, tail_ref = rest

    if seg is None:
        @pl.when(pl.program_id(1) == 0)
        def _init():
            tail_ref[...] = buf_ref[...]
        row8 = lax.broadcasted_iota(jnp.int32, (SUBLANES, D_FF), 0)
    else:
        pos = lax.broadcasted_iota(jnp.int32, (tm, D_FF), 0) % seg

    x = x_ref[0]
    h = _rms(x, g_ref[...]).astype(BF16)
    halves = []
    for half in range(2):
        cs = slice(half * D_FF, (half + 1) * D_FF)
        u = _dot(h, wup_ref[:, cs])
        r1 = pltpu.roll(u, 1, 0)
        r2 = pltpu.roll(u, 2, 0)
        if seg is None:
            hist = tail_ref[0, :, cs]
            top1 = jnp.where(row8 == 0, hist[1:2], r1[:SUBLANES])
            top2 = jnp.where(row8 == 0, hist[0:1], jnp.where(row8 == 1, hist[1:2], r2[:SUBLANES]))
            r1 = jnp.concatenate([top1, r1[SUBLANES:]], axis=0)
            r2 = jnp.concatenate([top2, r2[SUBLANES:]], axis=0)
            tail_ref[0, :, cs] = u[tm - 2:tm]
        else:
            hx = buf_ref[0, :, cs]
            r1 = jnp.where(pos == 0, pltpu.roll(hx, tm - 1, 0), r1)
            r2 = jnp.where(pos < 2, hx, r2)
            tail_ref[0, :, cs] = u
        cw = cw_ref[:, cs]
        halves.append(r2 * cw[0:1] + r1 * cw[1:2] + u * cw[2:3] + cb_ref[:, cs])
    act = (jax.nn.silu(halves[0]) * halves[1]).astype(BF16)
    xo = x + _dot(act, wdn_ref[...])
    if with_final:
        xo = _rms(xo, gfin_ref[...])
    xo_ref[0] = xo


def _ffn_layer(x, buf, g, w_up, conv_w, conv_b, w_down, *, tm, seg=None, final=None):
    B, T, D = x.shape
    x_spec = pl.BlockSpec((1, tm, D), lambda b, t: (b, t, 0))
    if seg is None:
        tail_spec = pl.BlockSpec((1, CONV_W - 1, 2 * D_FF), lambda b, t: (b, 0, 0))
        tail_shape = jax.ShapeDtypeStruct((B, CONV_W - 1, 2 * D_FF), F32)
    else:
        tail_spec = pl.BlockSpec((1, tm, 2 * D_FF), lambda b, t: (b, t, 0))
        tail_shape = jax.ShapeDtypeStruct((B, T, 2 * D_FF), F32)
    args = [x, buf, g.reshape(1, D), w_up, conv_w, conv_b.reshape(1, 2 * D_FF), w_down]
    in_specs = [x_spec, tail_spec, _const_spec((1, D)), _const_spec(w_up.shape), _const_spec(conv_w.shape),
                _const_spec((1, 2 * D_FF)), _const_spec(w_down.shape)]
    out_specs = [x_spec, tail_spec]
    out_shape = [jax.ShapeDtypeStruct((B, T, D), F32), tail_shape]
    if final is not None:
        args.append(final.reshape(1, D))
        in_specs.append(_const_spec((1, D)))
    return pl.pallas_call(
        functools.partial(_ffn_body, tm=tm, seg=seg, with_final=final is not None),
        grid=(B, T // tm),
        in_specs=in_specs,
        out_specs=out_specs,
        out_shape=out_shape,
        compiler_params=_params(),
        name="conv_ffn",
    )(*args)


def _expand_kv(a):
    rows = a.shape[0]
    lo = lax.broadcasted_iota(jnp.int32, (rows, 2 * HEAD_DIM), 1) < HEAD_DIM
    zero = jnp.zeros((rows, 2 * HEAD_DIM), F32)
    ones_l = jnp.where(lo, 1.0, 0.0)
    ones_r = 1.0 - ones_l

    def halves(c):
        sw = pltpu.roll(c, HEAD_DIM, 1)
        return (jnp.where(lo, c, zero), jnp.where(lo, zero, sw), jnp.where(lo, sw, zero), jnp.where(lo, zero, c))

    k0e, k0o, k1e, k1o = halves(a[:, :KV_W])
    v0e, v0o, v1e, v1o = halves(a[:, KV_W:])
    keys = jnp.concatenate([k0e, k0o, k1e, k1o], axis=1).astype(BF16)
    vals = jnp.concatenate([v0e, ones_l, v0o, ones_r, v1e, ones_l, v1o, ones_r], axis=1).astype(BF16)
    return keys, vals


def _swa_body(*refs, tq, nb, banded, make_kv):
    refs = list(refs)
    x_ref = refs.pop(0)
    if make_kv:
        gkv_ref, wkv_ref = refs.pop(0), refs.pop(0)
    else:
        kv_ref = refs.pop(0)
    past_ref = None if banded else refs.pop(0)
    g_ref, wq_ref, sink_ref, wo_ref, xo_ref = refs[:5]
    refs = refs[5:]
    if make_kv:
        kvo_ref = refs.pop(0)
    if banded:
        kx, wx = refs
    P = WINDOW
    PAIR = 2 * HEAD_DIM
    NP = ATT_GROUP // 2
    t = pl.program_id(1)

    if banded:
        @pl.when(t == 0)
        def _first():
            kx[0:P, :] = jnp.zeros((P, kx.shape[1]), BF16)
            wx[0:P, :] = jnp.zeros((P, wx.shape[1]), BF16)

        @pl.when(t > 0)
        def _carry():
            kx[0:P, :] = kx[tq:tq + P, :]
            wx[0:P, :] = wx[tq:tq + P, :]


    x = x_ref[...].reshape(nb * tq, D_MODEL)
    xn = x * lax.rsqrt(jnp.mean(x * x, axis=-1, keepdims=True) + EPS)
    if make_kv:
        kv_new = _dot((xn * gkv_ref[...]).astype(BF16), wkv_ref[...])
        kvo_ref[...] = kv_new.reshape(nb, tq, 2 * KV_W)
    else:
        kv_new = kv_ref[...].reshape(nb * tq, 2 * KV_W)
    if banded:
        kx[P:P + tq, :], wx[P:P + tq, :] = _expand_kv(kv_new)
    else:
        past_k, past_v = _expand_kv(past_ref[...].reshape(nb * P, 2 * KV_W))
        new_k, new_v = _expand_kv(kv_new)
    h = (xn * g_ref[...]).astype(BF16)
    q = (_dot(h, wq_ref[...]) * (HEAD_DIM ** -0.5)).astype(BF16)

    sub = 2 * CHUNK if banded else tq
    lk = sub + P
    lo = lax.broadcasted_iota(jnp.int32, (sub, PAIR), 1) < HEAD_DIM
    nt = (((1,), (1,)), ((), ()))
    outs = []
    for i in range(nb * tq // sub):
        qs = q[i * sub:(i + 1) * sub]
        if banded:
            keys = kx[i * sub:i * sub + lk, :]
            vals = wx[i * sub:i * sub + lk, :]
            ri = lax.broadcasted_iota(jnp.int32, (sub, lk), 0)
            ci = lax.broadcasted_iota(jnp.int32, (sub, lk), 1)
            off = ci - jnp.where(ri >= CHUNK, CHUNK, 0)
            valid = (off >= 0) & (off < WINDOW + CHUNK) & (ci + (t * tq + i * sub - P) >= 0)
        else:
            keys = jnp.concatenate([past_k[i * P:(i + 1) * P], new_k[i * sub:(i + 1) * sub]], axis=0)
            vals = jnp.concatenate([past_v[i * P:(i + 1) * P], new_v[i * sub:(i + 1) * sub]], axis=0)
        blocks = []
        for n in range(ATT_KV_HEADS):
            qp = jnp.concatenate([qs[:, (n * NP + j) * PAIR:(n * NP + j + 1) * PAIR] for j in range(NP)], axis=0)
            es, sts = [], []
            for par in range(2):
                kn = keys[:, (2 * n + par) * PAIR:(2 * n + par + 1) * PAIR]
                s_all = lax.dot_general(qp, kn, nt, preferred_element_type=F32)
                e_rows, st_rows = [], []
                for j in range(NP):
                    s = s_all[j * sub:(j + 1) * sub]
                    if banded:
                        s = jnp.where(valid, s, -jnp.inf)
                    sk = sink_ref[n * ATT_GROUP + 2 * j + par]
                    m = jnp.maximum(jnp.max(s, axis=-1, keepdims=True), sk)
                    e_rows.append(jnp.exp(s - m).astype(BF16))
                    st_rows.append(jnp.exp(sk - m))
                es.append(jnp.concatenate(e_rows, axis=0))
                sts.append(st_rows)
            xv = (_dot(es[0], vals[:, (2 * n) * 2 * PAIR:(2 * n + 1) * 2 * PAIR])
                  + _dot(es[1], vals[:, (2 * n + 1) * 2 * PAIR:(2 * n + 2) * 2 * PAIR]))
            for j in range(NP):
                xj = xv[j * sub:(j + 1) * sub]
                den = xj[:, PAIR:] + jnp.where(lo, sts[0][j], sts[1][j])
                blocks.append((xj[:, :PAIR] / den).astype(BF16))
        outs.append(jnp.concatenate(blocks, axis=1))
    o = outs[0] if len(outs) == 1 else jnp.concatenate(outs, axis=0)
    xo_ref[...] = (x + _dot(o, wo_ref[...])).reshape(nb, tq, D_MODEL)


def _swa_layer(x, kv, past, g, w_q, sinks, w_o, *, tq, nb):
    B, T, D = x.shape
    banded = past is None
    make_kv = isinstance(kv, tuple)
    x_spec = pl.BlockSpec((nb, tq, D), lambda b, t: (b, t, 0))
    kv_spec = pl.BlockSpec((nb, tq, 2 * KV_W), lambda b, t: (b, t, 0))
    out_specs, out_shape = [x_spec], [jax.ShapeDtypeStruct((B, T, D), F32)]
    if make_kv:
        args = [x, kv[0].reshape(1, D), kv[1]]
        in_specs = [x_spec, _const_spec((1, D)), _const_spec(kv[1].shape)]
        out_specs.append(kv_spec)
        out_shape.append(jax.ShapeDtypeStruct((B, T, 2 * KV_W), F32))
    else:
        args = [x, kv]
        in_specs = [x_spec, kv_spec]
    scratch = []
    if banded:
        scratch = [pltpu.VMEM((WINDOW + tq, 4 * 2 * HEAD_DIM), BF16), pltpu.VMEM((WINDOW + tq, 4 * 4 * HEAD_DIM), BF16)]
    else:
        args.append(past)
        in_specs.append(pl.BlockSpec((nb, WINDOW, 2 * KV_W), lambda b, t: (b, 0, 0)))
    args += [g.reshape(1, D), w_q, sinks, w_o]
    in_specs += [_const_spec((1, D)), _const_spec(w_q.shape),
                 pl.BlockSpec(memory_space=pltpu.SMEM), _const_spec(w_o.shape)]
    return pl.pallas_call(
        functools.partial(_swa_body, tq=tq, nb=nb, banded=banded, make_kv=make_kv),
        grid=(B // nb, T // tq),
        in_specs=in_specs,
        out_specs=out_specs,
        out_shape=out_shape,
        scratch_shapes=scratch,
        compiler_params=_params(),
        name="swa_layer",
    )(*args)


def _trunk(x, gla_s, ffn_buf, past, w, *, tm_gla, nb_streams, tm_ffn, tq):
    new_gla, new_buf = [], []
    kv = None
    for layer in range(DEPTH):
        if layer < N_A:
            x, s = _gla_layer(x, gla_s, layer, w["norm_mix"][layer], w["gla_w_in"][layer], w["gla_w_gl"][layer],
                              w["gla_w_g2"][layer], w["gla_b_g2"][layer], w["gla_g_onorm"][layer],
                              w["gla_w_out"][layer], tm=tm_gla, nb=nb_streams)
            new_gla.append(s)
        else:
            j = layer - N_A
            res = _swa_layer(x, (w["norm_kv"], w["w_kv"]) if j == 0 else kv, past, w["norm_mix"][layer],
                             w["att_w_q"][j], w["att_sinks"][j], w["att_w_o"][j], tq=tq, nb=nb_streams)
            x = res[0]
            if j == 0:
                kv = res[1]
        ffn_args = (w["norm_ffn"][layer], w["ffn_w_up"][layer], w["ffn_conv_w"][layer], w["ffn_conv_b"][layer],
                    w["ffn_w_down"][layer])
        ffn_opts = dict(final=w["norm_final"] if layer == DEPTH - 1 else None)
        B, T, D = x.shape
        if T >= tm_ffn:
            res = _ffn_layer(x, ffn_buf[layer], *ffn_args, tm=tm_ffn, **ffn_opts)
            tail = res[1]
        else:
            hx = jnp.pad(ffn_buf[layer], ((0, 0), (0, T - (CONV_W - 1)), (0, 0))).reshape(1, B * T, 2 * D_FF)
            res = _ffn_layer(x.reshape(1, B * T, D), hx, *ffn_args, tm=min(B * T, FFN_FOLD_ROWS), seg=T, **ffn_opts)
            res = [r.reshape(B, T, r.shape[-1]) for r in res]
            tail = res[1][:, T - (CONV_W - 1):, :]
        x = res[0]
        new_buf.append(tail)
    return x, jnp.stack(new_gla), kv, jnp.stack(new_buf)


def kernel(x_prompt, x_sample, state_gla, cache_k, cache_v, state_ffn_conv, norm_mix, norm_ffn, norm_kv, norm_final, gla_w_in, gla_w_g2, gla_b_g2, gla_g_onorm, gla_w_out, w_kv, att_w_q, att_sinks, att_w_o, ffn_w_up, ffn_conv_w, ffn_conv_b, ffn_w_down):
    n_main = 2 * QK_W + 2 * V_W
    w = dict(
        norm_mix=norm_mix, norm_ffn=norm_ffn, norm_kv=norm_kv, norm_final=norm_final,
        gla_w_in=gla_w_in[:, :, :n_main].astype(BF16), gla_w_gl=gla_w_in[:, :, n_main:].astype(BF16),
        gla_w_g2=gla_w_g2.astype(BF16), gla_b_g2=gla_b_g2, gla_g_onorm=gla_g_onorm,
        gla_w_out=gla_w_out.astype(BF16), w_kv=w_kv.astype(BF16), att_w_q=att_w_q.astype(BF16),
        att_sinks=att_sinks, att_w_o=att_w_o.astype(BF16), ffn_w_up=ffn_w_up.astype(BF16),
        ffn_conv_w=ffn_conv_w, ffn_conv_b=ffn_conv_b, ffn_w_down=ffn_w_down.astype(BF16))

    bp = x_prompt.shape[0]
    buf0 = jnp.zeros((DEPTH, bp, CONV_W - 1, 2 * D_FF), F32)
    y_p, gla_p, kv_p, buf_p = _trunk(x_prompt, None, buf0, None, w, tm_gla=512, nb_streams=1, tm_ffn=FFN_ROWS, tq=512)
    k_p = kv_p[:, -WINDOW:, :KV_W].reshape(bp, WINDOW, ATT_KV_HEADS, HEAD_DIM)
    v_p = kv_p[:, -WINDOW:, KV_W:].reshape(bp, WINDOW, ATT_KV_HEADS, HEAD_DIM)

    bs, ts = x_sample.shape[:2]
    rows = cache_k.shape[1]
    past = jnp.concatenate([cache_k.reshape(bs, rows, KV_W), cache_v.reshape(bs, rows, KV_W)], axis=-1)
    y_s, gla_s, kv_s, buf_s = _trunk(x_sample, state_gla, state_ffn_conv, past, w, tm_gla=ts, nb_streams=SHORT_STREAMS,
                                     tm_ffn=FFN_ROWS, tq=ts)
    k_s = jnp.concatenate([past[:, :, :KV_W], kv_s[:, :, :KV_W]], axis=1)[:, -rows:]
    v_s = jnp.concatenate([past[:, :, KV_W:], kv_s[:, :, KV_W:]], axis=1)[:, -rows:]
    k_s = k_s.reshape(bs, rows, ATT_KV_HEADS, HEAD_DIM)
    v_s = v_s.reshape(bs, rows, ATT_KV_HEADS, HEAD_DIM)
    return (y_p, y_s, gla_p, k_p, v_p, buf_p, gla_s, k_s, v_s, buf_s)
```

```python
import functools

import jax
import jax.numpy as jnp
from jax import lax
from jax.experimental import pallas as pl
from jax.experimental.pallas import tpu as pltpu

F32 = jnp.float32
BF16 = jnp.bfloat16

D_MODEL = 1024
DEPTH = 4
N_A = DEPTH // 2
CHUNK = 64
WINDOW = 128
GLA_HEADS = 4
GLA_DK = 128
GLA_DV = 256
GLA_RANK = 16
GLA_TAU = 16.0
GLA_BLOCK = 16
GLA_CHUNK = 128
QK_W = GLA_HEADS * GLA_DK
V_W = GLA_HEADS * GLA_DV
ATT_HEADS = 16
ATT_KV_HEADS = 2
ATT_GROUP = ATT_HEADS // ATT_KV_HEADS
HEAD_DIM = 64
KV_W = ATT_KV_HEADS * HEAD_DIM
D_FF = 2816
CONV_W = 3
EPS = 1e-6

VMEM_LIMIT_BYTES = 56 * 1024 * 1024
SHORT_STREAMS = 8
FFN_FOLD_ROWS = 256
FFN_COLS = D_FF
FFN_ROWS = 512


def _rms(x, g):
    ms = jnp.mean(x * x, axis=-1, keepdims=True)
    return x * lax.rsqrt(ms + EPS) * g


def _dot(a, b):
    return jnp.dot(a, b, preferred_element_type=F32)


def _const_spec(shape):
    return pl.BlockSpec(shape, lambda b, t: (0,) * len(shape), pipeline_mode=pl.Buffered(1))


def _params():
    return pltpu.CompilerParams(dimension_semantics=("arbitrary", "arbitrary"),
                                vmem_limit_bytes=VMEM_LIMIT_BYTES)


SUBLANES = 8
LOG2_E = 1.4426950408889634


def _cumsum_rows(a):
    n, w = a.shape
    g = n // SUBLANES
    x = a.reshape(g, SUBLANES, w)
    sub = lax.broadcasted_iota(jnp.int32, (g, SUBLANES, w), 1)
    d = 1
    while d < SUBLANES:
        x = x + jnp.where(sub >= d, pltpu.roll(x, d, 1), 0.0)
        d *= 2
    run = jnp.zeros((1, w), F32)
    out = []
    for i in range(g):
        out.append(x[i] + run)
        run = run + x[i, SUBLANES - 1:SUBLANES, :]
    return jnp.concatenate(out, axis=0)

def _gla_body(*refs, tm, nb, zero_state):
    if zero_state:
        x_ref, g_ref, win_ref, wgl_ref, wg2_ref, bg2_ref, gon_ref, wout_ref, xo_ref, s_ref = refs
    else:
        x_ref, s0_ref, g_ref, win_ref, wgl_ref, wg2_ref, bg2_ref, gon_ref, wout_ref, xo_ref, s_ref = refs
    C = GLA_CHUNK if nb == 1 else tm
    L = GLA_BLOCK
    nblk = C // L
    units = [(0, c * C) for c in range(tm // C)] if nb == 1 else [(s, s * C) for s in range(nb)]

    @pl.when(pl.program_id(1) == 0)
    def _init():
        s_ref[...] = jnp.zeros(s_ref.shape, F32) if zero_state else s0_ref[...]

    x = x_ref[...].reshape(nb * tm, D_MODEL)
    h = _rms(x, g_ref[...]).astype(BF16)

    gl = _dot(h, wgl_ref[...])
    z = _dot(gl.astype(BF16), wg2_ref[...]) + bg2_ref[...]
    la = (jnp.minimum(z, 0.0) - jnp.log(1.0 + jnp.exp(-jnp.abs(z)))) * (LOG2_E / GLA_TAU)

    p = _dot(h, win_ref[...])

    ri = lax.broadcasted_iota(jnp.int32, (C, C), 0)
    ci = lax.broadcasted_iota(jnp.int32, (C, C), 1)
    tril = ri >= ci
    gon = gon_ref[...]

    decay = []
    for _, r0 in units:
        b = _cumsum_rows(la[r0:r0 + C])
        b_end = b[C - 1:C]
        e_q = jnp.exp2(b)
        e_k = jnp.exp2(b_end - b)
        g_t = [jnp.transpose(jnp.broadcast_to(jnp.exp2(b_end[:, hd * GLA_DK:(hd + 1) * GLA_DK]), (GLA_DK, GLA_DK)))
               for hd in range(GLA_HEADS)]
        e_qi, e_ki = [], []
        for i in range(nblk):
            b_i0 = b[L * i - 1:L * i] if i > 0 else jnp.zeros((1, QK_W), F32)
            e_qi.append(jnp.exp2(b[L * i:L * (i + 1)] - b_i0))
            e_ki.append(jnp.exp2(b_i0 - b[:L * (i + 1)]))
        decay.append((e_q, e_k, g_t, e_qi, e_ki))

    gated = []
    for (slot, r0), (e_q, e_k, g_t, e_qi, e_ki) in zip(units, decay):
        sl = slice(r0, r0 + C)
        q = p[sl, 0:QK_W] * (GLA_DK ** -0.5)
        k = p[sl, QK_W:2 * QK_W]
        v = p[sl, 2 * QK_W:2 * QK_W + V_W]
        r = p[sl, 2 * QK_W + V_W:2 * QK_W + 2 * V_W]
        qt = (q * e_q).astype(BF16)
        kh = (k * e_k).astype(BF16)
        vb = v.astype(BF16)

        a_rows = []
        for i in range(nblk):
            qi = (q[L * i:L * (i + 1)] * e_qi[i]).astype(BF16)
            n = L * (i + 1)
            ki = (k[:n] * e_ki[i]).astype(BF16)
            if n < C:
                ki = jnp.concatenate([ki, jnp.zeros((C - n, QK_W), BF16)], axis=0)
            a_rows.append([
                lax.dot_general(qi[:, hd * GLA_DK:(hd + 1) * GLA_DK], ki[:, hd * GLA_DK:(hd + 1) * GLA_DK],
                                (((1,), (1,)), ((), ())), preferred_element_type=F32)
                for hd in range(GLA_HEADS)])

        og = []
        for hd in range(GLA_HEADS):
            ks = slice(hd * GLA_DK, (hd + 1) * GLA_DK)
            vs = slice(hd * GLA_DV, (hd + 1) * GLA_DV)
            att = jnp.concatenate([a_rows[i][hd] for i in range(nblk)], axis=0)
            att = jnp.where(tril, att, 0.0).astype(BF16)
            s_old = s_ref[slot, hd]
            o = _dot(qt[:, ks], s_old.astype(BF16)) + _dot(att, vb[:, vs])
            ktv = lax.dot_general(kh[:, ks], vb[:, vs], (((0,), (0,)), ((), ())), preferred_element_type=F32)
            s_ref[slot, hd] = s_old * jnp.concatenate([g_t[hd], g_t[hd]], axis=1) + ktv
            o = o * lax.rsqrt(jnp.mean(o * o, axis=-1, keepdims=True) + EPS) * gon
            og.append((o * jax.nn.silu(r[:, vs])).astype(BF16))
        og = jnp.concatenate(og, axis=1)
        if nb == 1:
            xo_ref[0, sl, :] = x[sl] + _dot(og, wout_ref[...])
        else:
            gated.append(og)
    if nb > 1:
        y = x + _dot(jnp.concatenate(gated, axis=0), wout_ref[...])
        xo_ref[...] = y.reshape(nb, tm, D_MODEL)


def _gla_layer(x, s0, layer, g, w_in, w_gl, w_g2, b_g2, g_on, w_out, *, tm, nb):
    B, T, D = x.shape
    x_spec = pl.BlockSpec((nb, tm, D), lambda b, t: (b, t, 0))
    st_block = (nb, GLA_HEADS, GLA_DK, GLA_DV)
    args, in_specs = [x], [x_spec]
    if s0 is not None:
        args.append(s0)
        in_specs.append(pl.BlockSpec((None,) + st_block, lambda b, t: (layer, b, 0, 0, 0)))
    args += [g.reshape(1, D), w_in, w_gl, w_g2, b_g2.reshape(1, QK_W), g_on.reshape(1, GLA_DV), w_out]
    in_specs += [_const_spec((1, D)), _const_spec(w_in.shape), _const_spec(w_gl.shape), _const_spec(w_g2.shape),
                 _const_spec((1, QK_W)), _const_spec((1, GLA_DV)), _const_spec(w_out.shape)]
    return pl.pallas_call(
        functools.partial(_gla_body, tm=tm, nb=nb, zero_state=s0 is None),
        grid=(B // nb, T // tm),
        in_specs=in_specs,
        out_specs=[x_spec, pl.BlockSpec(st_block, lambda b, t: (b, 0, 0, 0))],
        out_shape=[jax.ShapeDtypeStruct((B, T, D), F32),
                   jax.ShapeDtypeStruct((B, GLA_HEADS, GLA_DK, GLA_DV), F32)],
        compiler_params=_params(),
        name="gla_layer",
    )(*args)


def _ffn_body(*refs, tm, seg, with_final):
    x_ref, buf_ref, g_ref, wup_ref, cw_ref, cb_ref, wdn_ref = refs[:7]
    rest = list(refs[7:])
    if with_final:
        gfin_ref = rest.pop(0)
    xo_ref, tail_ref = rest

    if seg is None:
        @pl.when(pl.program_id(1) == 0)
        def _init():
            tail_ref[...] = buf_ref[...]
        row8 = lax.broadcasted_iota(jnp.int32, (SUBLANES, FFN_COLS), 0)
    else:
        pos = lax.broadcasted_iota(jnp.int32, (tm, FFN_COLS), 0) % seg

    x = x_ref[0]
    h = _rms(x, g_ref[...]).astype(BF16)

    def conv(cs):
        u = _dot(h, wup_ref[:, cs])
        r1 = pltpu.roll(u, 1, 0)
        r2 = pltpu.roll(u, 2, 0)
        if seg is None:
            hist = tail_ref[0, :, cs]
            top1 = jnp.where(row8 == 0, hist[1:2], r1[:SUBLANES])
            top2 = jnp.where(row8 == 0, hist[0:1], jnp.where(row8 == 1, hist[1:2], r2[:SUBLANES]))
            r1 = jnp.concatenate([top1, r1[SUBLANES:]], axis=0)
            r2 = jnp.concatenate([top2, r2[SUBLANES:]], axis=0)
            tail_ref[0, :, cs] = u[tm - 2:tm]
        else:
            hx = buf_ref[0, :, cs]
            r1 = jnp.where(pos == 0, pltpu.roll(hx, tm - 1, 0), r1)
            r2 = jnp.where(pos < 2, hx, r2)
            tail_ref[0, :, cs] = u
        cw = cw_ref[:, cs]
        return r2 * cw[0:1] + r1 * cw[1:2] + u * cw[2:3] + cb_ref[:, cs]

    act = []
    for j in range(D_FF // FFN_COLS):
        gate_c = conv(slice(j * FFN_COLS, (j + 1) * FFN_COLS))
        val_c = conv(slice(D_FF + j * FFN_COLS, D_FF + (j + 1) * FFN_COLS))
        act.append((jax.nn.silu(gate_c) * val_c).astype(BF16))
    xo = x + _dot(jnp.concatenate(act, axis=1), wdn_ref[...])
    if with_final:
        xo = _rms(xo, gfin_ref[...])
    xo_ref[0] = xo


def _ffn_layer(x, buf, g, w_up, conv_w, conv_b, w_down, *, tm, seg=None, final=None):
    B, T, D = x.shape
    x_spec = pl.BlockSpec((1, tm, D), lambda b, t: (b, t, 0))
    if seg is None:
        tail_spec = pl.BlockSpec((1, CONV_W - 1, 2 * D_FF), lambda b, t: (b, 0, 0))
        tail_shape = jax.ShapeDtypeStruct((B, CONV_W - 1, 2 * D_FF), F32)
    else:
        tail_spec = pl.BlockSpec((1, tm, 2 * D_FF), lambda b, t: (b, t, 0))
        tail_shape = jax.ShapeDtypeStruct((B, T, 2 * D_FF), F32)
    args = [x, buf, g.reshape(1, D), w_up, conv_w, conv_b.reshape(1, 2 * D_FF), w_down]
    in_specs = [x_spec, tail_spec, _const_spec((1, D)), _const_spec(w_up.shape), _const_spec(conv_w.shape),
                _const_spec((1, 2 * D_FF)), _const_spec(w_down.shape)]
    out_specs = [x_spec, tail_spec]
    out_shape = [jax.ShapeDtypeStruct((B, T, D), F32), tail_shape]
    if final is not None:
        args.append(final.reshape(1, D))
        in_specs.append(_const_spec((1, D)))
    return pl.pallas_call(
        functools.partial(_ffn_body, tm=tm, seg=seg, with_final=final is not None),
        grid=(B, T // tm),
        in_specs=in_specs,
        out_specs=out_specs,
        out_shape=out_shape,
        compiler_params=_params(),
        name="conv_ffn",
    )(*args)


def _expand_kv(a):
    rows = a.shape[0]
    lo = lax.broadcasted_iota(jnp.int32, (rows, 2 * HEAD_DIM), 1) < HEAD_DIM
    zero = jnp.zeros((rows, 2 * HEAD_DIM), F32)
    ones_l = jnp.where(lo, 1.0, 0.0)
    ones_r = 1.0 - ones_l

    def halves(c):
        sw = pltpu.roll(c, HEAD_DIM, 1)
        return (jnp.where(lo, c, zero), jnp.where(lo, zero, sw), jnp.where(lo, sw, zero), jnp.where(lo, zero, c))

    k0e, k0o, k1e, k1o = halves(a[:, :KV_W])
    v0e, v0o, v1e, v1o = halves(a[:, KV_W:])
    keys = jnp.concatenate([k0e, k0o, k1e, k1o], axis=1).astype(BF16)
    vals = jnp.concatenate([v0e, ones_l, v0o, ones_r, v1e, ones_l, v1o, ones_r], axis=1).astype(BF16)
    return keys, vals


def _swa_body(*refs, tq, nb, banded, make_kv):
    refs = list(refs)
    x_ref = refs.pop(0)
    if make_kv:
        gkv_ref, wkv_ref = refs.pop(0), refs.pop(0)
    else:
        kv_ref = refs.pop(0)
    past_ref = None if banded else refs.pop(0)
    g_ref, wq_ref, sink_ref, wo_ref, xo_ref = refs[:5]
    refs = refs[5:]
    if make_kv:
        kvo_ref = refs.pop(0)
    if banded:
        kx, wx = refs
    P = WINDOW
    PAIR = 2 * HEAD_DIM
    NP = ATT_GROUP // 2
    t = pl.program_id(1)

    if banded:
        @pl.when(t == 0)
        def _first():
            kx[0:P, :] = jnp.zeros((P, kx.shape[1]), BF16)
            wx[0:P, :] = jnp.zeros((P, wx.shape[1]), BF16)

        @pl.when(t > 0)
        def _carry():
            kx[0:P, :] = kx[tq:tq + P, :]
            wx[0:P, :] = wx[tq:tq + P, :]


    x = x_ref[...].reshape(nb * tq, D_MODEL)
    xn = x * lax.rsqrt(jnp.mean(x * x, axis=-1, keepdims=True) + EPS)
    if make_kv:
        kv_new = _dot((xn * gkv_ref[...]).astype(BF16), wkv_ref[...])
        kvo_ref[...] = kv_new.reshape(nb, tq, 2 * KV_W)
    else:
        kv_new = kv_ref[...].reshape(nb * tq, 2 * KV_W)
    if banded:
        kx[P:P + tq, :], wx[P:P + tq, :] = _expand_kv(kv_new)
    else:
        past_k, past_v = _expand_kv(past_ref[...].reshape(nb * P, 2 * KV_W))
        new_k, new_v = _expand_kv(kv_new)
    h = (xn * g_ref[...]).astype(BF16)
    q = (_dot(h, wq_ref[...]) * (HEAD_DIM ** -0.5)).astype(BF16)

    sub = 2 * CHUNK if banded else tq
    lk = sub + P
    lo = lax.broadcasted_iota(jnp.int32, (sub, PAIR), 1) < HEAD_DIM
    nt = (((1,), (1,)), ((), ()))
    outs = []
    for i in range(nb * tq // sub):
        qs = q[i * sub:(i + 1) * sub]
        if banded:
            keys = kx[i * sub:i * sub + lk, :]
            vals = wx[i * sub:i * sub + lk, :]
            ri = lax.broadcasted_iota(jnp.int32, (sub, lk), 0)
            ci = lax.broadcasted_iota(jnp.int32, (sub, lk), 1)
            off = ci - jnp.where(ri >= CHUNK, CHUNK, 0)
            valid = (off >= 0) & (off < WINDOW + CHUNK) & (ci + (t * tq + i * sub - P) >= 0)
        else:
            keys = jnp.concatenate([past_k[i * P:(i + 1) * P], new_k[i * sub:(i + 1) * sub]], axis=0)
            vals = jnp.concatenate([past_v[i * P:(i + 1) * P], new_v[i * sub:(i + 1) * sub]], axis=0)
        blocks = []
        for n in range(ATT_KV_HEADS):
            qp = jnp.concatenate([qs[:, (n * NP + j) * PAIR:(n * NP + j + 1) * PAIR] for j in range(NP)], axis=0)
            es, sts = [], []
            for par in range(2):
                kn = keys[:, (2 * n + par) * PAIR:(2 * n + par + 1) * PAIR]
                s_all = lax.dot_general(qp, kn, nt, preferred_element_type=F32)
                e_rows, st_rows = [], []
                for j in range(NP):
                    s = s_all[j * sub:(j + 1) * sub]
                    if banded:
                        s = jnp.where(valid, s, -jnp.inf)
                    sk = sink_ref[n * ATT_GROUP + 2 * j + par]
                    m = jnp.maximum(jnp.max(s, axis=-1, keepdims=True), sk)
                    e_rows.append(jnp.exp(s - m).astype(BF16))
                    st_rows.append(jnp.exp(sk - m))
                es.append(jnp.concatenate(e_rows, axis=0))
                sts.append(st_rows)
            xv = (_dot(es[0], vals[:, (2 * n) * 2 * PAIR:(2 * n + 1) * 2 * PAIR])
                  + _dot(es[1], vals[:, (2 * n + 1) * 2 * PAIR:(2 * n + 2) * 2 * PAIR]))
            for j in range(NP):
                xj = xv[j * sub:(j + 1) * sub]
                den = xj[:, PAIR:] + jnp.where(lo, sts[0][j], sts[1][j])
                blocks.append((xj[:, :PAIR] / den).astype(BF16))
        outs.append(jnp.concatenate(blocks, axis=1))
    o = outs[0] if len(outs) == 1 else jnp.concatenate(outs, axis=0)
    xo_ref[...] = (x + _dot(o, wo_ref[...])).reshape(nb, tq, D_MODEL)


def _swa_layer(x, kv, past, g, w_q, sinks, w_o, *, tq, nb):
    B, T, D = x.shape
    banded = past is None
    make_kv = isinstance(kv, tuple)
    x_spec = pl.BlockSpec((nb, tq, D), lambda b, t: (b, t, 0))
    kv_spec = pl.BlockSpec((nb, tq, 2 * KV_W), lambda b, t: (b, t, 0))
    out_specs, out_shape = [x_spec], [jax.ShapeDtypeStruct((B, T, D), F32)]
    if make_kv:
        args = [x, kv[0].reshape(1, D), kv[1]]
        in_specs = [x_spec, _const_spec((1, D)), _const_spec(kv[1].shape)]
        out_specs.append(kv_spec)
        out_shape.append(jax.ShapeDtypeStruct((B, T, 2 * KV_W), F32))
    else:
        args = [x, kv]
        in_specs = [x_spec, kv_spec]
    scratch = []
    if banded:
        scratch = [pltpu.VMEM((WINDOW + tq, 4 * 2 * HEAD_DIM), BF16), pltpu.VMEM((WINDOW + tq, 4 * 4 * HEAD_DIM), BF16)]
    else:
        args.append(past)
        in_specs.append(pl.BlockSpec((nb, WINDOW, 2 * KV_W), lambda b, t: (b, 0, 0)))
    args += [g.reshape(1, D), w_q, sinks, w_o]
    in_specs += [_const_spec((1, D)), _const_spec(w_q.shape),
                 pl.BlockSpec(memory_space=pltpu.SMEM), _const_spec(w_o.shape)]
    return pl.pallas_call(
        functools.partial(_swa_body, tq=tq, nb=nb, banded=banded, make_kv=make_kv),
        grid=(B // nb, T // tq),
        in_specs=in_specs,
        out_specs=out_specs,
        out_shape=out_shape,
        scratch_shapes=scratch,
        compiler_params=_params(),
        name="swa_layer",
    )(*args)


def _trunk(x, gla_s, ffn_buf, past, w, *, tm_gla, nb_streams, tm_ffn, tq):
    new_gla, new_buf = [], []
    kv = None
    for layer in range(DEPTH):
        if layer < N_A:
            x, s = _gla_layer(x, gla_s, layer, w["norm_mix"][layer], w["gla_w_in"][layer], w["gla_w_gl"][layer],
                              w["gla_w_g2"][layer], w["gla_b_g2"][layer], w["gla_g_onorm"][layer],
                              w["gla_w_out"][layer], tm=tm_gla, nb=nb_streams)
            new_gla.append(s)
        else:
            j = layer - N_A
            res = _swa_layer(x, (w["norm_kv"], w["w_kv"]) if j == 0 else kv, past, w["norm_mix"][layer],
                             w["att_w_q"][j], w["att_sinks"][j], w["att_w_o"][j], tq=tq, nb=nb_streams)
            x = res[0]
            if j == 0:
                kv = res[1]
        ffn_args = (w["norm_ffn"][layer], w["ffn_w_up"][layer], w["ffn_conv_w"][layer], w["ffn_conv_b"][layer],
                    w["ffn_w_down"][layer])
        ffn_opts = dict(final=w["norm_final"] if layer == DEPTH - 1 else None)
        B, T, D = x.shape
        if T >= tm_ffn:
            res = _ffn_layer(x, ffn_buf[layer], *ffn_args, tm=tm_ffn, **ffn_opts)
            tail = res[1]
        else:
            hx = jnp.pad(ffn_buf[layer], ((0, 0), (0, T - (CONV_W - 1)), (0, 0))).reshape(1, B * T, 2 * D_FF)
            res = _ffn_layer(x.reshape(1, B * T, D), hx, *ffn_args, tm=min(B * T, FFN_FOLD_ROWS), seg=T, **ffn_opts)
            res = [r.reshape(B, T, r.shape[-1]) for r in res]
            tail = res[1][:, T - (CONV_W - 1):, :]
        x = res[0]
        new_buf.append(tail)
    return x, jnp.stack(new_gla), kv, jnp.stack(new_buf)


def kernel(x_prompt, x_sample, state_gla, cache_k, cache_v, state_ffn_conv, norm_mix, norm_ffn, norm_kv, norm_final, gla_w_in, gla_w_g2, gla_b_g2, gla_g_onorm, gla_w_out, w_kv, att_w_q, att_sinks, att_w_o, ffn_w_up, ffn_conv_w, ffn_conv_b, ffn_w_down):
    n_main = 2 * QK_W + 2 * V_W
    w = dict(
        norm_mix=norm_mix, norm_ffn=norm_ffn, norm_kv=norm_kv, norm_final=norm_final,
        gla_w_in=gla_w_in[:, :, :n_main].astype(BF16), gla_w_gl=gla_w_in[:, :, n_main:].astype(BF16),
        gla_w_g2=gla_w_g2.astype(BF16), gla_b_g2=gla_b_g2, gla_g_onorm=gla_g_onorm,
        gla_w_out=gla_w_out.astype(BF16), w_kv=w_kv.astype(BF16), att_w_q=att_w_q.astype(BF16),
        att_sinks=att_sinks, att_w_o=att_w_o.astype(BF16), ffn_w_up=ffn_w_up.astype(BF16),
        ffn_conv_w=ffn_conv_w, ffn_conv_b=ffn_conv_b, ffn_w_down=ffn_w_down.astype(BF16))

    bp = x_prompt.shape[0]
    buf0 = jnp.zeros((DEPTH, bp, CONV_W - 1, 2 * D_FF), F32)
    y_p, gla_p, kv_p, buf_p = _trunk(x_prompt, None, buf0, None, w, tm_gla=1024, nb_streams=1, tm_ffn=FFN_ROWS, tq=1024)
    k_p = kv_p[:, -WINDOW:, :KV_W].reshape(bp, WINDOW, ATT_KV_HEADS, HEAD_DIM)
    v_p = kv_p[:, -WINDOW:, KV_W:].reshape(bp, WINDOW, ATT_KV_HEADS, HEAD_DIM)

    bs, ts = x_sample.shape[:2]
    rows = cache_k.shape[1]
    past = jnp.concatenate([cache_k.reshape(bs, rows, KV_W), cache_v.reshape(bs, rows, KV_W)], axis=-1)
    y_s, gla_s, kv_s, buf_s = _trunk(x_sample, state_gla, state_ffn_conv, past, w, tm_gla=ts, nb_streams=SHORT_STREAMS,
                                     tm_ffn=FFN_ROWS, tq=ts)
    k_s = jnp.concatenate([past[:, :, :KV_W], kv_s[:, :, :KV_W]], axis=1)[:, -rows:]
    v_s = jnp.concatenate([past[:, :, KV_W:], kv_s[:, :, KV_W:]], axis=1)[:, -rows:]
    k_s = k_s.reshape(bs, rows, ATT_KV_HEADS, HEAD_DIM)
    v_s = v_s.reshape(bs, rows, ATT_KV_HEADS, HEAD_DIM)
    return (y_p, y_s, gla_p, k_p, v_p, buf_p, gla_s, k_s, v_s, buf_s)
```

```python
import functools

import jax
import jax.numpy as jnp
from jax import lax
from jax.experimental import pallas as pl
from jax.experimental.pallas import tpu as pltpu

F32 = jnp.float32
BF16 = jnp.bfloat16

D_MODEL = 1024
DEPTH = 4
N_A = DEPTH // 2
CHUNK = 64
WINDOW = 128
GLA_HEADS = 4
GLA_DK = 128
GLA_DV = 256
GLA_RANK = 16
GLA_TAU = 16.0
GLA_BLOCK = 16
GLA_CHUNK = 128
QK_W = GLA_HEADS * GLA_DK
V_W = GLA_HEADS * GLA_DV
ATT_HEADS = 16
ATT_KV_HEADS = 2
ATT_GROUP = ATT_HEADS // ATT_KV_HEADS
HEAD_DIM = 64
KV_W = ATT_KV_HEADS * HEAD_DIM
D_FF = 2816
CONV_W = 3
EPS = 1e-6

VMEM_LIMIT_BYTES = 56 * 1024 * 1024
SHORT_STREAMS = 8
FFN_FOLD_ROWS = 256
FFN_COLS = D_FF
FFN_ROWS = 512


def _rms(x, g):
    ms = jnp.mean(x * x, axis=-1, keepdims=True)
    return x * lax.rsqrt(ms + EPS) * g


def _dot(a, b):
    return jnp.dot(a, b, preferred_element_type=F32)


def _const_spec(shape):
    return pl.BlockSpec(shape, lambda b, t: (0,) * len(shape), pipeline_mode=pl.Buffered(1))


def _params():
    return pltpu.CompilerParams(dimension_semantics=("arbitrary", "arbitrary"),
                                vmem_limit_bytes=VMEM_LIMIT_BYTES)


SUBLANES = 8
LOG2_E = 1.4426950408889634


def _cumsum_rows(a):
    n, w = a.shape
    g = n // SUBLANES
    x = a.reshape(g, SUBLANES, w)
    sub = lax.broadcasted_iota(jnp.int32, (g, SUBLANES, w), 1)
    d = 1
    while d < SUBLANES:
        x = x + jnp.where(sub >= d, pltpu.roll(x, d, 1), 0.0)
        d *= 2
    run = jnp.zeros((1, w), F32)
    out = []
    for i in range(g):
        out.append(x[i] + run)
        run = run + x[i, SUBLANES - 1:SUBLANES, :]
    return jnp.concatenate(out, axis=0)

def _gla_body(*refs, tm, nb, zero_state):
    if zero_state:
        x_ref, g_ref, win_ref, wgl_ref, wg2_ref, bg2_ref, gon_ref, wout_ref, xo_ref, s_ref = refs
    else:
        x_ref, s0_ref, g_ref, win_ref, wgl_ref, wg2_ref, bg2_ref, gon_ref, wout_ref, xo_ref, s_ref = refs
    C = GLA_CHUNK if nb == 1 else tm
    L = GLA_BLOCK
    nblk = C // L
    units = [(0, c * C) for c in range(tm // C)] if nb == 1 else [(s, s * C) for s in range(nb)]

    @pl.when(pl.program_id(1) == 0)
    def _init():
        s_ref[...] = jnp.zeros(s_ref.shape, F32) if zero_state else s0_ref[...]

    x = x_ref[...].reshape(nb * tm, D_MODEL)
    h = _rms(x, g_ref[...]).astype(BF16)

    gl = _dot(h, wgl_ref[...])
    z = _dot(gl.astype(BF16), wg2_ref[...]) + bg2_ref[...]
    la = (jnp.minimum(z, 0.0) - jnp.log(1.0 + jnp.exp(-jnp.abs(z)))) * (LOG2_E / GLA_TAU)

    p = _dot(h, win_ref[...])

    ri = lax.broadcasted_iota(jnp.int32, (C, C), 0)
    ci = lax.broadcasted_iota(jnp.int32, (C, C), 1)
    tril = ri >= ci
    gon = gon_ref[...]

    decay = []
    for _, r0 in units:
        b = _cumsum_rows(la[r0:r0 + C])
        b_end = b[C - 1:C]
        e_q = jnp.exp2(b)
        e_k = jnp.exp2(b_end - b)
        g_t = [jnp.transpose(jnp.broadcast_to(jnp.exp2(b_end[:, hd * GLA_DK:(hd + 1) * GLA_DK]), (GLA_DK, GLA_DK)))
               for hd in range(GLA_HEADS)]
        e_qi, e_ki = [], []
        for i in range(nblk):
            b_i0 = b[L * i - 1:L * i] if i > 0 else jnp.zeros((1, QK_W), F32)
            e_qi.append(jnp.exp2(b[L * i:L * (i + 1)] - b_i0))
            e_ki.append(jnp.exp2(b_i0 - b[:L * (i + 1)]))
        decay.append((e_q, e_k, g_t, e_qi, e_ki))

    gated = []
    for (slot, r0), (e_q, e_k, g_t, e_qi, e_ki) in zip(units, decay):
        sl = slice(r0, r0 + C)
        q = p[sl, 0:QK_W] * (GLA_DK ** -0.5)
        k = p[sl, QK_W:2 * QK_W]
        v = p[sl, 2 * QK_W:2 * QK_W + V_W]
        r = p[sl, 2 * QK_W + V_W:2 * QK_W + 2 * V_W]
        qt = (q * e_q).astype(BF16)
        kh = (k * e_k).astype(BF16)
        vb = v.astype(BF16)

        a_rows = []
        for i in range(nblk):
            qi = (q[L * i:L * (i + 1)] * e_qi[i]).astype(BF16)
            n = L * (i + 1)
            ki = (k[:n] * e_ki[i]).astype(BF16)
            if n < C:
                ki = jnp.concatenate([ki, jnp.zeros((C - n, QK_W), BF16)], axis=0)
            a_rows.append([
                lax.dot_general(qi[:, hd * GLA_DK:(hd + 1) * GLA_DK], ki[:, hd * GLA_DK:(hd + 1) * GLA_DK],
                                (((1,), (1,)), ((), ())), preferred_element_type=F32)
                for hd in range(GLA_HEADS)])

        og = []
        for hd in range(GLA_HEADS):
            ks = slice(hd * GLA_DK, (hd + 1) * GLA_DK)
            vs = slice(hd * GLA_DV, (hd + 1) * GLA_DV)
            att = jnp.concatenate([a_rows[i][hd] for i in range(nblk)], axis=0)
            att = jnp.where(tril, att, 0.0).astype(BF16)
            s_old = s_ref[slot, hd]
            o = _dot(jnp.concatenate([qt[:, ks], att], axis=1),
                     jnp.concatenate([s_old.astype(BF16), vb[:, vs]], axis=0))
            ktv = lax.dot_general(kh[:, ks], vb[:, vs], (((0,), (0,)), ((), ())), preferred_element_type=F32)
            s_ref[slot, hd] = s_old * jnp.concatenate([g_t[hd], g_t[hd]], axis=1) + ktv
            o = o * lax.rsqrt(jnp.mean(o * o, axis=-1, keepdims=True) + EPS) * gon
            og.append((o * jax.nn.silu(r[:, vs])).astype(BF16))
        og = jnp.concatenate(og, axis=1)
        if nb == 1:
            xo_ref[0, sl, :] = x[sl] + _dot(og, wout_ref[...])
        else:
            gated.append(og)
    if nb > 1:
        y = x + _dot(jnp.concatenate(gated, axis=0), wout_ref[...])
        xo_ref[...] = y.reshape(nb, tm, D_MODEL)


def _gla_layer(x, s0, layer, g, w_in, w_gl, w_g2, b_g2, g_on, w_out, *, tm, nb):
    B, T, D = x.shape
    x_spec = pl.BlockSpec((nb, tm, D), lambda b, t: (b, t, 0))
    st_block = (nb, GLA_HEADS, GLA_DK, GLA_DV)
    args, in_specs = [x], [x_spec]
    if s0 is not None:
        args.append(s0)
        in_specs.append(pl.BlockSpec((None,) + st_block, lambda b, t: (layer, b, 0, 0, 0)))
    args += [g.reshape(1, D), w_in, w_gl, w_g2, b_g2.reshape(1, QK_W), g_on.reshape(1, GLA_DV), w_out]
    in_specs += [_const_spec((1, D)), _const_spec(w_in.shape), _const_spec(w_gl.shape), _const_spec(w_g2.shape),
                 _const_spec((1, QK_W)), _const_spec((1, GLA_DV)), _const_spec(w_out.shape)]
    return pl.pallas_call(
        functools.partial(_gla_body, tm=tm, nb=nb, zero_state=s0 is None),
        grid=(B // nb, T // tm),
        in_specs=in_specs,
        out_specs=[x_spec, pl.BlockSpec(st_block, lambda b, t: (b, 0, 0, 0))],
        out_shape=[jax.ShapeDtypeStruct((B, T, D), F32),
                   jax.ShapeDtypeStruct((B, GLA_HEADS, GLA_DK, GLA_DV), F32)],
        compiler_params=_params(),
        name="gla_layer",
    )(*args)


def _ffn_body(*refs, tm, seg, with_final):
    x_ref, buf_ref, g_ref, wup_ref, cw_ref, cb_ref, wdn_ref = refs[:7]
    rest = list(refs[7:])
    if with_final:
        gfin_ref = rest.pop(0)
    xo_ref, tail_ref = rest[:2]
    if seg is None:
        ubuf = rest[2]

    if seg is None:
        @pl.when(pl.program_id(1) == 0)
        def _init():
            tail_ref[...] = buf_ref[...]
    else:
        pos = lax.broadcasted_iota(jnp.int32, (tm, FFN_COLS), 0) % seg

    x = x_ref[0]
    h = _rms(x, g_ref[...]).astype(BF16)

    def conv(cs):
        u = _dot(h, wup_ref[:, cs])
        if seg is None:
            ubuf[SUBLANES - 2:SUBLANES, cs] = tail_ref[0, :, cs]
            ubuf[SUBLANES:SUBLANES + tm, cs] = u
            r1 = ubuf[SUBLANES - 1:SUBLANES - 1 + tm, cs]
            r2 = ubuf[SUBLANES - 2:SUBLANES - 2 + tm, cs]
            tail_ref[0, :, cs] = u[tm - 2:tm]
        else:
            hx = buf_ref[0, :, cs]
            r1 = jnp.where(pos == 0, pltpu.roll(hx, tm - 1, 0), pltpu.roll(u, 1, 0))
            r2 = jnp.where(pos < 2, hx, pltpu.roll(u, 2, 0))
            tail_ref[0, :, cs] = u
        cw = cw_ref[:, cs]
        return r2 * cw[0:1] + r1 * cw[1:2] + u * cw[2:3] + cb_ref[:, cs]

    act = []
    for j in range(D_FF // FFN_COLS):
        gate_c = conv(slice(j * FFN_COLS, (j + 1) * FFN_COLS))
        val_c = conv(slice(D_FF + j * FFN_COLS, D_FF + (j + 1) * FFN_COLS))
        act.append((jax.nn.silu(gate_c) * val_c).astype(BF16))
    xo = x + _dot(jnp.concatenate(act, axis=1), wdn_ref[...])
    if with_final:
        xo = _rms(xo, gfin_ref[...])
    xo_ref[0] = xo


def _ffn_layer(x, buf, g, w_up, conv_w, conv_b, w_down, *, tm, seg=None, final=None):
    B, T, D = x.shape
    x_spec = pl.BlockSpec((1, tm, D), lambda b, t: (b, t, 0))
    if seg is None:
        tail_spec = pl.BlockSpec((1, CONV_W - 1, 2 * D_FF), lambda b, t: (b, 0, 0))
        tail_shape = jax.ShapeDtypeStruct((B, CONV_W - 1, 2 * D_FF), F32)
    else:
        tail_spec = pl.BlockSpec((1, tm, 2 * D_FF), lambda b, t: (b, t, 0))
        tail_shape = jax.ShapeDtypeStruct((B, T, 2 * D_FF), F32)
    args = [x, buf, g.reshape(1, D), w_up, conv_w, conv_b.reshape(1, 2 * D_FF), w_down]
    in_specs = [x_spec, tail_spec, _const_spec((1, D)), _const_spec(w_up.shape), _const_spec(conv_w.shape),
                _const_spec((1, 2 * D_FF)), _const_spec(w_down.shape)]
    out_specs = [x_spec, tail_spec]
    out_shape = [jax.ShapeDtypeStruct((B, T, D), F32), tail_shape]
    if final is not None:
        args.append(final.reshape(1, D))
        in_specs.append(_const_spec((1, D)))
    return pl.pallas_call(
        functools.partial(_ffn_body, tm=tm, seg=seg, with_final=final is not None),
        scratch_shapes=[pltpu.VMEM((SUBLANES + tm, 2 * D_FF), F32)] if seg is None else [],
        grid=(B, T // tm),
        in_specs=in_specs,
        out_specs=out_specs,
        out_shape=out_shape,
        compiler_params=_params(),
        name="conv_ffn",
    )(*args)


def _expand_kv(a):
    rows = a.shape[0]
    lo = lax.broadcasted_iota(jnp.int32, (rows, 2 * HEAD_DIM), 1) < HEAD_DIM
    zero = jnp.zeros((rows, 2 * HEAD_DIM), F32)
    ones_l = jnp.where(lo, 1.0, 0.0)
    ones_r = 1.0 - ones_l

    def halves(c):
        sw = pltpu.roll(c, HEAD_DIM, 1)
        return (jnp.where(lo, c, zero), jnp.where(lo, zero, sw), jnp.where(lo, sw, zero), jnp.where(lo, zero, c))

    k0e, k0o, k1e, k1o = halves(a[:, :KV_W])
    v0e, v0o, v1e, v1o = halves(a[:, KV_W:])
    keys = jnp.concatenate([k0e, k0o, k1e, k1o], axis=1).astype(BF16)
    vals = jnp.concatenate([v0e, ones_l, v0o, ones_r, v1e, ones_l, v1o, ones_r], axis=1).astype(BF16)
    return keys, vals


def _swa_body(*refs, tq, nb, banded, make_kv):
    refs = list(refs)
    x_ref = refs.pop(0)
    if make_kv:
        gkv_ref, wkv_ref = refs.pop(0), refs.pop(0)
    else:
        kv_ref = refs.pop(0)
    past_ref = None if banded else refs.pop(0)
    g_ref, wq_ref, sink_ref, wo_ref, xo_ref = refs[:5]
    refs = refs[5:]
    if make_kv:
        kvo_ref = refs.pop(0)
    if banded:
        kx, wx = refs
    P = WINDOW
    PAIR = 2 * HEAD_DIM
    NP = ATT_GROUP // 2
    t = pl.program_id(1)

    if banded:
        @pl.when(t == 0)
        def _first():
            kx[0:P, :] = jnp.zeros((P, kx.shape[1]), BF16)
            wx[0:P, :] = jnp.zeros((P, wx.shape[1]), BF16)

        @pl.when(t > 0)
        def _carry():
            kx[0:P, :] = kx[tq:tq + P, :]
            wx[0:P, :] = wx[tq:tq + P, :]


    x = x_ref[...].reshape(nb * tq, D_MODEL)
    xn = x * lax.rsqrt(jnp.mean(x * x, axis=-1, keepdims=True) + EPS)
    if make_kv:
        kv_new = _dot((xn * gkv_ref[...]).astype(BF16), wkv_ref[...])
        kvo_ref[...] = kv_new.reshape(nb, tq, 2 * KV_W)
    else:
        kv_new = kv_ref[...].reshape(nb * tq, 2 * KV_W)
    if banded:
        kx[P:P + tq, :], wx[P:P + tq, :] = _expand_kv(kv_new)
    else:
        past_k, past_v = _expand_kv(past_ref[...].reshape(nb * P, 2 * KV_W))
        new_k, new_v = _expand_kv(kv_new)
    h = (xn * g_ref[...]).astype(BF16)
    q = (_dot(h, wq_ref[...]) * (HEAD_DIM ** -0.5)).astype(BF16)

    sub = 2 * CHUNK if banded else tq
    lk = sub + P
    lo = lax.broadcasted_iota(jnp.int32, (sub, PAIR), 1) < HEAD_DIM
    nt = (((1,), (1,)), ((), ()))
    outs = []
    for i in range(nb * tq // sub):
        qs = q[i * sub:(i + 1) * sub]
        if banded:
            keys = kx[i * sub:i * sub + lk, :]
            vals = wx[i * sub:i * sub + lk, :]
            ri = lax.broadcasted_iota(jnp.int32, (sub, lk), 0)
            ci = lax.broadcasted_iota(jnp.int32, (sub, lk), 1)
            off = ci - jnp.where(ri >= CHUNK, CHUNK, 0)
            valid = (off >= 0) & (off < WINDOW + CHUNK) & (ci + (t * tq + i * sub - P) >= 0)
        else:
            keys = jnp.concatenate([past_k[i * P:(i + 1) * P], new_k[i * sub:(i + 1) * sub]], axis=0)
            vals = jnp.concatenate([past_v[i * P:(i + 1) * P], new_v[i * sub:(i + 1) * sub]], axis=0)
        blocks = []
        for n in range(ATT_KV_HEADS):
            qp = jnp.concatenate([qs[:, (n * NP + j) * PAIR:(n * NP + j + 1) * PAIR] for j in range(NP)], axis=0)
            es, sts = [], []
            for par in range(2):
                kn = keys[:, (2 * n + par) * PAIR:(2 * n + par + 1) * PAIR]
                s_all = lax.dot_general(qp, kn, nt, preferred_element_type=F32)
                e_rows, st_rows = [], []
                for j in range(NP):
                    s = s_all[j * sub:(j + 1) * sub]
                    if banded:
                        s = jnp.where(valid, s, -jnp.inf)
                    sk = sink_ref[n * ATT_GROUP + 2 * j + par]
                    m = jnp.maximum(jnp.max(s, axis=-1, keepdims=True), sk)
                    e_rows.append(jnp.exp(s - m).astype(BF16))
                    st_rows.append(jnp.exp(sk - m))
                es.append(jnp.concatenate(e_rows, axis=0))
                sts.append(st_rows)
            xv = (_dot(es[0], vals[:, (2 * n) * 2 * PAIR:(2 * n + 1) * 2 * PAIR])
                  + _dot(es[1], vals[:, (2 * n + 1) * 2 * PAIR:(2 * n + 2) * 2 * PAIR]))
            for j in range(NP):
                xj = xv[j * sub:(j + 1) * sub]
                den = xj[:, PAIR:] + jnp.where(lo, sts[0][j], sts[1][j])
                blocks.append((xj[:, :PAIR] / den).astype(BF16))
        outs.append(jnp.concatenate(blocks, axis=1))
    o = outs[0] if len(outs) == 1 else jnp.concatenate(outs, axis=0)
    xo_ref[...] = (x + _dot(o, wo_ref[...])).reshape(nb, tq, D_MODEL)


def _swa_layer(x, kv, past, g, w_q, sinks, w_o, *, tq, nb):
    B, T, D = x.shape
    banded = past is None
    make_kv = isinstance(kv, tuple)
    x_spec = pl.BlockSpec((nb, tq, D), lambda b, t: (b, t, 0))
    kv_spec = pl.BlockSpec((nb, tq, 2 * KV_W), lambda b, t: (b, t, 0))
    out_specs, out_shape = [x_spec], [jax.ShapeDtypeStruct((B, T, D), F32)]
    if make_kv:
        args = [x, kv[0].reshape(1, D), kv[1]]
        in_specs = [x_spec, _const_spec((1, D)), _const_spec(kv[1].shape)]
        out_specs.append(kv_spec)
        out_shape.append(jax.ShapeDtypeStruct((B, T, 2 * KV_W), F32))
    else:
        args = [x, kv]
        in_specs = [x_spec, kv_spec]
    scratch = []
    if banded:
        scratch = [pltpu.VMEM((WINDOW + tq, 4 * 2 * HEAD_DIM), BF16), pltpu.VMEM((WINDOW + tq, 4 * 4 * HEAD_DIM), BF16)]
    else:
        args.append(past)
        in_specs.append(pl.BlockSpec((nb, WINDOW, 2 * KV_W), lambda b, t: (b, 0, 0)))
    args += [g.reshape(1, D), w_q, sinks, w_o]
    in_specs += [_const_spec((1, D)), _const_spec(w_q.shape),
                 pl.BlockSpec(memory_space=pltpu.SMEM), _const_spec(w_o.shape)]
    return pl.pallas_call(
        functools.partial(_swa_body, tq=tq, nb=nb, banded=banded, make_kv=make_kv),
        grid=(B // nb, T // tq),
        in_specs=in_specs,
        out_specs=out_specs,
        out_shape=out_shape,
        scratch_shapes=scratch,
        compiler_params=_params(),
        name="swa_layer",
    )(*args)


def _trunk(x, gla_s, ffn_buf, past, w, *, tm_gla, nb_streams, tm_ffn, tq):
    new_gla, new_buf = [], []
    kv = None
    for layer in range(DEPTH):
        if layer < N_A:
            x, s = _gla_layer(x, gla_s, layer, w["norm_mix"][layer], w["gla_w_in"][layer], w["gla_w_gl"][layer],
                              w["gla_w_g2"][layer], w["gla_b_g2"][layer], w["gla_g_onorm"][layer],
                              w["gla_w_out"][layer], tm=tm_gla, nb=nb_streams)
            new_gla.append(s)
        else:
            j = layer - N_A
            res = _swa_layer(x, (w["norm_kv"], w["w_kv"]) if j == 0 else kv, past, w["norm_mix"][layer],
                             w["att_w_q"][j], w["att_sinks"][j], w["att_w_o"][j], tq=tq, nb=nb_streams)
            x = res[0]
            if j == 0:
                kv = res[1]
        ffn_args = (w["norm_ffn"][layer], w["ffn_w_up"][layer], w["ffn_conv_w"][layer], w["ffn_conv_b"][layer],
                    w["ffn_w_down"][layer])
        ffn_opts = dict(final=w["norm_final"] if layer == DEPTH - 1 else None)
        B, T, D = x.shape
        if T >= tm_ffn:
            res = _ffn_layer(x, ffn_buf[layer], *ffn_args, tm=tm_ffn, **ffn_opts)
            tail = res[1]
        else:
            hx = jnp.pad(ffn_buf[layer], ((0, 0), (0, T - (CONV_W - 1)), (0, 0))).reshape(1, B * T, 2 * D_FF)
            res = _ffn_layer(x.reshape(1, B * T, D), hx, *ffn_args, tm=min(B * T, FFN_FOLD_ROWS), seg=T, **ffn_opts)
            res = [r.reshape(B, T, r.shape[-1]) for r in res]
            tail = res[1][:, T - (CONV_W - 1):, :]
        x = res[0]
        new_buf.append(tail)
    return x, jnp.stack(new_gla), kv, jnp.stack(new_buf)


def kernel(x_prompt, x_sample, state_gla, cache_k, cache_v, state_ffn_conv, norm_mix, norm_ffn, norm_kv, norm_final, gla_w_in, gla_w_g2, gla_b_g2, gla_g_onorm, gla_w_out, w_kv, att_w_q, att_sinks, att_w_o, ffn_w_up, ffn_conv_w, ffn_conv_b, ffn_w_down):
    n_main = 2 * QK_W + 2 * V_W
    w = dict(
        norm_mix=norm_mix, norm_ffn=norm_ffn, norm_kv=norm_kv, norm_final=norm_final,
        gla_w_in=gla_w_in[:, :, :n_main].astype(BF16), gla_w_gl=gla_w_in[:, :, n_main:].astype(BF16),
        gla_w_g2=gla_w_g2.astype(BF16), gla_b_g2=gla_b_g2, gla_g_onorm=gla_g_onorm,
        gla_w_out=gla_w_out.astype(BF16), w_kv=w_kv.astype(BF16), att_w_q=att_w_q.astype(BF16),
        att_sinks=att_sinks, att_w_o=att_w_o.astype(BF16), ffn_w_up=ffn_w_up.astype(BF16),
        ffn_conv_w=ffn_conv_w, ffn_conv_b=ffn_conv_b, ffn_w_down=ffn_w_down.astype(BF16))

    bp = x_prompt.shape[0]
    buf0 = jnp.zeros((DEPTH, bp, CONV_W - 1, 2 * D_FF), F32)
    y_p, gla_p, kv_p, buf_p = _trunk(x_prompt, None, buf0, None, w, tm_gla=1024, nb_streams=1, tm_ffn=FFN_ROWS, tq=1024)
    k_p = kv_p[:, -WINDOW:, :KV_W].reshape(bp, WINDOW, ATT_KV_HEADS, HEAD_DIM)
    v_p = kv_p[:, -WINDOW:, KV_W:].reshape(bp, WINDOW, ATT_KV_HEADS, HEAD_DIM)

    bs, ts = x_sample.shape[:2]
    rows = cache_k.shape[1]
    past = jnp.concatenate([cache_k.reshape(bs, rows, KV_W), cache_v.reshape(bs, rows, KV_W)], axis=-1)
    y_s, gla_s, kv_s, buf_s = _trunk(x_sample, state_gla, state_ffn_conv, past, w, tm_gla=ts, nb_streams=SHORT_STREAMS,
                                     tm_ffn=FFN_ROWS, tq=ts)
    k_s = jnp.concatenate([past[:, :, :KV_W], kv_s[:, :, :KV_W]], axis=1)[:, -rows:]
    v_s = jnp.concatenate([past[:, :, KV_W:], kv_s[:, :, KV_W:]], axis=1)[:, -rows:]
    k_s = k_s.reshape(bs, rows, ATT_KV_HEADS, HEAD_DIM)
    v_s = v_s.reshape(bs, rows, ATT_KV_HEADS, HEAD_DIM)
    return (y_p, y_s, gla_p, k_p, v_p, buf_p, gla_s, k_s, v_s, buf_s)
```
